```python
import jax, jax.numpy as jnp
from jax import lax
import numpy as np

D_MODEL = 1024
BATCH = 8
SEQ = 4096
DEPTH = 4

N_HEADS = 8
HEAD_DIM = 128
N_KV_HEADS = 2
ROPE_FRACTION_DIV = 4
ROPE_THETA = 500000.0
IDX_HEADS = 8
IDX_DIM = 64
TOPK_MAX = 256
Q_BLOCK = 128
CONF_WIDTH = D_MODEL
CONF_KERNEL = 31
SC_WIDTH = D_MODEL
SC_KERNEL = 3
FFN_DIM = 2816
FFN_KERNEL = 3
N_BRANCH = 3
NORM_EPS = 1e-6

Q_COLS = N_HEADS * HEAD_DIM
KV_COLS = N_KV_HEADS * HEAD_DIM
IQ_COLS = IDX_HEADS * IDX_DIM
IK_COLS = IDX_DIM
IW_COLS = IDX_HEADS
A_COLS = 2 * CONF_WIDTH
C_COLS = 3 * SC_WIDTH
G_COLS = N_BRANCH * D_MODEL
_IN_SIZES = (Q_COLS, KV_COLS, KV_COLS, IQ_COLS, IK_COLS, IW_COLS, A_COLS, C_COLS, G_COLS)
N_IN = sum(_IN_SIZES)
IN_SPLITS = tuple(sum(_IN_SIZES[:i + 1]) for i in range(len(_IN_SIZES) - 1))

kernel_name = "hybrid_gated_conformer_dsa_shortconv"


def rms_norm(x, g):
    xf = x.astype(jnp.float32)
    r = lax.rsqrt(jnp.mean(xf * xf, axis=-1, keepdims=True) + NORM_EPS)
    return (xf * r * g.astype(jnp.float32)).astype(x.dtype)


def layer_norm(x, g, b):
    xf = x.astype(jnp.float32)
    mu = jnp.mean(xf, axis=-1, keepdims=True)
    xc = xf - mu
    r = lax.rsqrt(jnp.mean(xc * xc, axis=-1, keepdims=True) + NORM_EPS)
    return (xc * r * g.astype(jnp.float32) + b.astype(jnp.float32)).astype(x.dtype)


def causal_dwconv(x, w):
    K, C = w.shape
    return lax.conv_general_dilated(
        x, w[:, None, :].astype(x.dtype), window_strides=(1,), padding=[(K - 1, 0)],
        dimension_numbers=("NWC", "WIO", "NWC"), feature_group_count=C)


def rope_tables(seq_len, rot_dim):
    pos = jnp.arange(seq_len, dtype=jnp.float32)
    inv_freq = jnp.power(ROPE_THETA, -jnp.arange(0, rot_dim, 2, dtype=jnp.float32) / rot_dim)
    ang = pos[:, None] * inv_freq[None, :]
    return jnp.cos(ang), jnp.sin(ang)


def partial_rope(x, cos, sin):
    half = cos.shape[-1]
    xf = x.astype(jnp.float32)
    x1, x2, xp = xf[..., :half], xf[..., half:2 * half], xf[..., 2 * half:]
    c = cos[None, :, None, :]
    s = sin[None, :, None, :]
    out = jnp.concatenate([x1 * c - x2 * s, x2 * c + x1 * s, xp], axis=-1)
    return out.astype(x.dtype)


def dsa_attention(q, k, v, iq, ik, iw):
    B, S, H, dh = q.shape
    G = k.shape[2]
    rep = H // G
    k_sel = min(TOPK_MAX, S // 4)
    n_blocks = S // Q_BLOCK
    key_pos = jnp.arange(S)
    scale = HEAD_DIM ** -0.5
    gather = jax.vmap(lambda arr, idx: arr[idx])

    def block(i):
        t0 = i * Q_BLOCK
        qb = lax.dynamic_slice_in_dim(q, t0, Q_BLOCK, axis=1)
        iqb = lax.dynamic_slice_in_dim(iq, t0, Q_BLOCK, axis=1)
        iwb = lax.dynamic_slice_in_dim(iw, t0, Q_BLOCK, axis=1)
        q_pos = t0 + jnp.arange(Q_BLOCK)
        rel = jax.nn.relu(jnp.einsum('bthd,bsd->bths', iqb, ik).astype(jnp.float32))
        score = jnp.einsum('bths,bth->bts', rel, iwb.astype(jnp.float32))
        causal = key_pos[None, :] <= q_pos[:, None]
        score = jnp.where(causal[None], score, -jnp.inf)
        _, idx = lax.top_k(score, k_sel)
        valid = idx <= q_pos[None, :, None]
        flat = idx.reshape(B, Q_BLOCK * k_sel)
        kg = gather(k, flat).reshape(B, Q_BLOCK, k_sel, G, dh)
        vg = gather(v, flat).reshape(B, Q_BLOCK, k_sel, G, dh)
        qg = qb.reshape(B, Q_BLOCK, G, rep, dh)
        logits = jnp.einsum('btgrd,btkgd->btgrk', qg, kg).astype(jnp.float32) * scale
        logits = jnp.where(valid[:, :, None, None, :], logits, -jnp.inf)
        p = jax.nn.softmax(logits, axis=-1).astype(v.dtype)
        o = jnp.einsum('btgrk,btkgd->btgrd', p, vg)
        return o.reshape(B, Q_BLOCK, H * dh)

    out = lax.map(block, jnp.arange(n_blocks))
    return out.transpose(1, 0, 2, 3).reshape(B, S, H * dh)


def hybrid_layer(x, rope_attn, rope_idx, norm1_g, w_in, q_norm_g, k_norm_g, w_attn_out,
                 conf_conv_w, conf_conv_b, conf_ln_g, conf_ln_b, w_conf_out,
                 sc_conv_w, w_sc_out, w_o, norm2_g, w_up, ffn_conv_w, ffn_conv_b, w_down):
    B, S, _ = x.shape
    h = rms_norm(x, norm1_g)
    proj = h @ w_in
    q, k, v, iq, ik, iw, a_in, c_in, g_in = jnp.split(proj, IN_SPLITS, axis=-1)

    q = partial_rope(rms_norm(q.reshape(B, S, N_HEADS, HEAD_DIM), q_norm_g), *rope_attn)
    k = partial_rope(rms_norm(k.reshape(B, S, N_KV_HEADS, HEAD_DIM), k_norm_g), *rope_attn)
    v = v.reshape(B, S, N_KV_HEADS, HEAD_DIM)
    iq = partial_rope(iq.reshape(B, S, IDX_HEADS, IDX_DIM), *rope_idx)
    ik = partial_rope(ik[:, :, None, :], *rope_idx)[:, :, 0, :]
    iw = iw * (IDX_HEADS ** -0.5 * IDX_DIM ** -0.5)
    y_attn = dsa_attention(q, k, v, iq, ik, iw) @ w_attn_out

    a_val, a_gate = jnp.split(a_in, 2, axis=-1)
    a = causal_dwconv(a_val * jax.nn.sigmoid(a_gate), conf_conv_w) + conf_conv_b
    y_conf = jax.nn.silu(layer_norm(a, conf_ln_g, conf_ln_b)) @ w_conf_out

    c_b, c_c, c_x = jnp.split(c_in, 3, axis=-1)
    y_sc = (c_b * causal_dwconv(c_c * c_x, sc_conv_w)) @ w_sc_out

    gates = jax.nn.sigmoid(g_in.reshape(B, S, N_BRANCH, D_MODEL).astype(jnp.float32)).astype(x.dtype)
    merged = gates[:, :, 0] * y_conf + gates[:, :, 1] * y_attn + gates[:, :, 2] * y_sc
    x = x + merged @ w_o

    u = causal_dwconv(rms_norm(x, norm2_g) @ w_up, ffn_conv_w) + ffn_conv_b
    u_gate, u_val = jnp.split(u, 2, axis=-1)
    return x + (jax.nn.silu(u_gate) * u_val) @ w_down


def setup_inputs(seed: int = 0) -> dict:
    key = jax.random.key(seed)
    ks = jax.random.split(key, 20)
    f32 = jnp.float32

    def nrm(k, shape, scale):
        return jax.random.normal(k, shape, f32) * scale

    L = DEPTH
    return {
        "x": nrm(ks[0], (BATCH, SEQ, D_MODEL), 1.0),
        "norm1_g": 1.0 + nrm(ks[1], (L, D_MODEL), 0.05),
        "w_in": nrm(ks[2], (L, D_MODEL, N_IN), D_MODEL ** -0.5),
        "q_norm_g": 1.0 + nrm(ks[3], (L, HEAD_DIM), 0.05),
        "k_norm_g": 1.0 + nrm(ks[4], (L, HEAD_DIM), 0.05),
        "w_attn_out": nrm(ks[5], (L, Q_COLS, D_MODEL), Q_COLS ** -0.5),
        "conf_conv_w": nrm(ks[6], (L, CONF_KERNEL, CONF_WIDTH), CONF_KERNEL ** -0.5),
        "conf_conv_b": nrm(ks[7], (L, CONF_WIDTH), 0.01),
        "conf_ln_g": 1.0 + nrm(ks[8], (L, CONF_WIDTH), 0.05),
        "conf_ln_b": nrm(ks[9], (L, CONF_WIDTH), 0.01),
        "w_conf_out": nrm(ks[10], (L, CONF_WIDTH, D_MODEL), CONF_WIDTH ** -0.5),
        "sc_conv_w": nrm(ks[11], (L, SC_KERNEL, SC_WIDTH), SC_KERNEL ** -0.5),
        "w_sc_out": nrm(ks[12], (L, SC_WIDTH, D_MODEL), SC_WIDTH ** -0.5),
        "w_o": nrm(ks[13], (L, D_MODEL, D_MODEL), D_MODEL ** -0.5),
        "norm2_g": 1.0 + nrm(ks[14], (L, D_MODEL), 0.05),
        "w_up": nrm(ks[15], (L, D_MODEL, 2 * FFN_DIM), D_MODEL ** -0.5),
        "ffn_conv_w": nrm(ks[16], (L, FFN_KERNEL, 2 * FFN_DIM), FFN_KERNEL ** -0.5),
        "ffn_conv_b": nrm(ks[17], (L, 2 * FFN_DIM), 0.01),
        "w_down": nrm(ks[18], (L, FFN_DIM, D_MODEL), FFN_DIM ** -0.5),
    }


def reference(x, norm1_g, w_in, q_norm_g, k_norm_g, w_attn_out, conf_conv_w, conf_conv_b,
              conf_ln_g, conf_ln_b, w_conf_out, sc_conv_w, w_sc_out, w_o, norm2_g, w_up,
              ffn_conv_w, ffn_conv_b, w_down):
    S = x.shape[1]
    rope_attn = rope_tables(S, HEAD_DIM // ROPE_FRACTION_DIV)
    rope_idx = rope_tables(S, IDX_DIM // ROPE_FRACTION_DIV)
    for l in range(DEPTH):
        x = hybrid_layer(x, rope_attn, rope_idx, norm1_g[l], w_in[l], q_norm_g[l], k_norm_g[l],
                         w_attn_out[l], conf_conv_w[l], conf_conv_b[l], conf_ln_g[l], conf_ln_b[l],
                         w_conf_out[l], sc_conv_w[l], w_sc_out[l], w_o[l], norm2_g[l], w_up[l],
                         ffn_conv_w[l], ffn_conv_b[l], w_down[l])
    return x
```

```python
import functools

import jax
import jax.numpy as jnp
from jax import lax
from jax.experimental import pallas as pl
from jax.experimental.pallas import tpu as pltpu

D_MODEL = 1024
N_HEADS = 8
HEAD_DIM = 128
N_KV_HEADS = 2
HEADS_PER_KV = N_HEADS // N_KV_HEADS
ROPE_FRACTION_DIV = 4
ROPE_THETA = 500000.0
IDX_HEADS = 8
IDX_DIM = 64
TOPK_MAX = 256
CONF_KERNEL = 31
SC_KERNEL = 3
FFN_DIM = 2816
FFN_KERNEL = 3
NORM_EPS = 1e-6

Q_COLS = N_HEADS * HEAD_DIM
KV_COLS = N_KV_HEADS * HEAD_DIM
IQ_COLS = IDX_HEADS * IDX_DIM
IN_SIZES = (Q_COLS, KV_COLS, KV_COLS, IQ_COLS, IDX_DIM, IDX_HEADS, 2 * D_MODEL, 3 * D_MODEL, 3 * D_MODEL)

LANES = 128
VMEM_LIMIT_BYTES = 56 * 1024 * 1024
MASK_NEG = -1e30
INT_MIN = -(2 ** 31)

F32 = jnp.float32
BF16 = jnp.bfloat16
NT_DIMS = (((1,), (1,)), ((), ()))


def _cparams(*sem):
    return pltpu.CompilerParams(dimension_semantics=sem, vmem_limit_bytes=VMEM_LIMIT_BYTES)


def _rms_rows(x, g):
    ms = jnp.mean(x * x, axis=-1, keepdims=True)
    return x * lax.rsqrt(ms + NORM_EPS) * g


def _rope(y, c, s1, s2, half):
    return y * c + pltpu.roll(y, half, 1) * s1 + pltpu.roll(y, LANES - half, 1) * s2


def _proj_attn_kernel(x_ref, g1_ref, w_ref, qg_ref, kg_ref, ca_ref, s1a_ref, s2a_ref,
                      ci_ref, s1i_ref, s2i_ref,
                      q_out, k_out, v_out, iq_out, ik_out, iw_out):
    h = _rms_rows(x_ref[...], g1_ref[...]).astype(BF16)
    ca, s1a, s2a = ca_ref[...], s1a_ref[...], s2a_ref[...]
    ci, s1i, s2i = ci_ref[...], s1i_ref[...], s2i_ref[...]
    half_a = HEAD_DIM // ROPE_FRACTION_DIV // 2
    half_i = IDX_DIM // ROPE_FRACTION_DIV // 2

    def cols(j, n=1):
        return jnp.dot(h, w_ref[:, j * LANES:(j + n) * LANES], preferred_element_type=F32)

    qscale = HEAD_DIM ** -0.5
    for hd in range(N_HEADS):
        y = _rope(_rms_rows(cols(hd), qg_ref[...]), ca, s1a, s2a, half_a)
        q_out[:, hd * LANES:(hd + 1) * LANES] = (y * qscale).astype(BF16)
    base = N_HEADS
    for hd in range(N_KV_HEADS):
        y = _rope(_rms_rows(cols(base + hd), kg_ref[...]), ca, s1a, s2a, half_a)
        k_out[:, hd * LANES:(hd + 1) * LANES] = y.astype(BF16)
    base += N_KV_HEADS
    v_out[...] = cols(base, N_KV_HEADS).astype(BF16)
    base += N_KV_HEADS
    for hd in range(IDX_HEADS):
        y = _rope(cols(base + hd), ci, s1i, s2i, half_i)
        iq_out[:, hd * LANES:(hd + 1) * LANES] = y.astype(BF16)
    base += IDX_HEADS
    ik_out[...] = _rope(cols(base), ci, s1i, s2i, half_i).astype(BF16)
    iw_out[...] = cols(base + 1) * (IDX_HEADS ** -0.5 * IDX_DIM ** -0.5)


def _proj_a_kernel(x_ref, g1_ref, w_ref, a_out):
    h = _rms_rows(x_ref[...], g1_ref[...]).astype(BF16)
    cw = 2 * LANES
    for j in range(D_MODEL // cw):
        val = jnp.dot(h, w_ref[:, j * cw:(j + 1) * cw], preferred_element_type=F32)
        gate = jnp.dot(h, w_ref[:, D_MODEL + j * cw:D_MODEL + (j + 1) * cw], preferred_element_type=F32)
        a_out[:, j * cw:(j + 1) * cw] = (val * jax.nn.sigmoid(gate)).astype(BF16)


def _proj_c_kernel(x_ref, g1_ref, w_ref, cb_out, ccx_out):
    h = _rms_rows(x_ref[...], g1_ref[...]).astype(BF16)
    cw = 2 * LANES
    for j in range(D_MODEL // cw):
        cb = jnp.dot(h, w_ref[:, j * cw:(j + 1) * cw], preferred_element_type=F32)
        cc = jnp.dot(h, w_ref[:, D_MODEL + j * cw:D_MODEL + (j + 1) * cw], preferred_element_type=F32)
        cx = jnp.dot(h, w_ref[:, 2 * D_MODEL + j * cw:2 * D_MODEL + (j + 1) * cw], preferred_element_type=F32)
        cb_out[:, j * cw:(j + 1) * cw] = cb.astype(BF16)
        ccx_out[:, j * cw:(j + 1) * cw] = (cc * cx).astype(BF16)


def _proj_g_kernel(x_ref, g1_ref, w_ref, g_out):
    h = _rms_rows(x_ref[...], g1_ref[...]).astype(BF16)
    cw = 2 * LANES
    for j in range(3 * D_MODEL // cw):
        g = jnp.dot(h, w_ref[:, j * cw:(j + 1) * cw], preferred_element_type=F32)
        g_out[:, j * cw:(j + 1) * cw] = jax.nn.sigmoid(g).astype(BF16)


def _row_spec(tm, n):
    return pl.BlockSpec((tm, n), lambda i: (i, 0))


def _const_spec(shape):
    return pl.BlockSpec(shape, lambda i: (0,) * len(shape))


def _input_projections(x2, g1, w_att, w_a, w_c, w_g, qg, kg, tabs_a, tabs_i, seq, tm):
    m = x2.shape[0]
    grid = (m // tm,)
    tpb = seq // tm
    tab_spec = pl.BlockSpec((tm, LANES), lambda i: (i % tpb, 0))
    n_att = w_att.shape[1]
    q, k, v, iq, ik, iw = pl.pallas_call(
        _proj_attn_kernel,
        grid=grid,
        in_specs=[_row_spec(tm, D_MODEL), _const_spec((1, D_MODEL)), _const_spec((D_MODEL, n_att)),
                  _const_spec((1, LANES)), _const_spec((1, LANES))] + [tab_spec] * 6,
        out_specs=[_row_spec(tm, Q_COLS), _row_spec(tm, KV_COLS), _row_spec(tm, KV_COLS),
                   _row_spec(tm, IDX_HEADS * LANES), _row_spec(tm, LANES), _row_spec(tm, LANES)],
        out_shape=[jax.ShapeDtypeStruct((m, Q_COLS), BF16), jax.ShapeDtypeStruct((m, KV_COLS), BF16),
                   jax.ShapeDtypeStruct((m, KV_COLS), BF16), jax.ShapeDtypeStruct((m, IDX_HEADS * LANES), BF16),
                   jax.ShapeDtypeStruct((m, LANES), BF16), jax.ShapeDtypeStruct((m, LANES), F32)],
        compiler_params=_cparams("parallel"),
        name="proj_attn",
    )(x2, g1, w_att, qg, kg, *tabs_a, *tabs_i)
    a_act = pl.pallas_call(
        _proj_a_kernel, grid=grid,
        in_specs=[_row_spec(tm, D_MODEL), _const_spec((1, D_MODEL)), _const_spec((D_MODEL, 2 * D_MODEL))],
        out_specs=_row_spec(tm, D_MODEL),
        out_shape=jax.ShapeDtypeStruct((m, D_MODEL), BF16),
        compiler_params=_cparams("parallel"), name="proj_a",
    )(x2, g1, w_a)
    cb, ccx = pl.pallas_call(
        _proj_c_kernel, grid=grid,
        in_specs=[_row_spec(tm, D_MODEL), _const_spec((1, D_MODEL)), _const_spec((D_MODEL, 3 * D_MODEL))],
        out_specs=[_row_spec(tm, D_MODEL), _row_spec(tm, D_MODEL)],
        out_shape=[jax.ShapeDtypeStruct((m, D_MODEL), BF16)] * 2,
        compiler_params=_cparams("parallel"), name="proj_c",
    )(x2, g1, w_c)
    gates = pl.pallas_call(
        _proj_g_kernel, grid=grid,
        in_specs=[_row_spec(tm, D_MODEL), _const_spec((1, D_MODEL)), _const_spec((D_MODEL, 3 * D_MODEL))],
        out_specs=_row_spec(tm, 3 * D_MODEL),
        out_shape=jax.ShapeDtypeStruct((m, 3 * D_MODEL), BF16),
        compiler_params=_cparams("parallel"), name="proj_g",
    )(x2, g1, w_g)
    return q, k, v, iq, ik, iw, a_act, cb, ccx, gates


def _select_kernel(iq_ref, ik_ref, iw_ref, bias_ref, keys_ref, p_ref, *, tq, seq, ksel):
    qi = pl.program_id(1)
    nck = qi + 1
    n_chunks = seq // tq
    iw_t = iw_ref[0].T
    t_idx = qi * tq + lax.broadcasted_iota(jnp.int32, (tq, tq), 1)
    row_iota = lax.broadcasted_iota(jnp.int32, (tq, tq), 0)

    def chunk_start(c):
        return pl.multiple_of(c * tq, tq)

    def score_chunk(c, carry):
        k0 = chunk_start(c)
        ikc = ik_ref[0, pl.ds(k0, tq), :]
        acc = jnp.zeros((tq, tq), F32)
        for hd in range(IDX_HEADS):
            iqh = iq_ref[0, :, hd * LANES:(hd + 1) * LANES]
            rel = lax.dot_general(ikc, iqh, NT_DIMS, preferred_element_type=F32)
            acc = acc + jnp.maximum(rel, 0.0) * iw_t[hd:hd + 1, :]
        bits = lax.bitcast_convert_type(acc, jnp.int32)
        key = jnp.where(bits < 0, -(bits & 0x7FFFFFFF), bits)
        key = jnp.where(k0 + row_iota <= t_idx, key, INT_MIN)
        keys_ref[pl.ds(k0, tq), :] = key
        return carry

    lax.fori_loop(0, nck, score_chunk, 0)

    def count(pred):
        def body(c, acc8):
            k0 = chunk_start(c)
            m = jnp.where(pred(keys_ref[pl.ds(k0, tq), :], k0), 1.0, 0.0)
            return acc8 + m.reshape(tq // 8, 8, tq).sum(axis=0)
        acc8 = lax.fori_loop(0, nck, body, jnp.zeros((8, tq), F32))
        return acc8.sum(axis=0, keepdims=True)

    kf = float(ksel)

    def bit_body(i, t_u):
        cand_u = t_u | jnp.left_shift(jnp.int32(1), 31 - i)
        cand_s = cand_u ^ INT_MIN
        cnt = count(lambda kk, k0: kk >= cand_s)
        return jnp.where(cnt >= kf, cand_u, t_u)

    thr = lax.fori_loop(0, 32, bit_body, jnp.zeros((1, tq), jnp.int32)) ^ INT_MIN

    c_gt = count(lambda kk, k0: kk > thr)
    c_eq = count(lambda kk, k0: kk == thr)
    need = kf - c_gt
    has_thr = thr != INT_MIN
    excess = jnp.logical_and(c_eq > need, has_thr)

    p_ref[...] = jnp.where(has_thr, seq, -1).astype(jnp.int32)

    @pl.when(jnp.max(jnp.where(excess, 1.0, 0.0)) > 0.0)
    def _():
        nbits = (seq - 1).bit_length()

        def tie_body(i, p):
            cand = p | jnp.left_shift(jnp.int32(1), nbits - 1 - i)
            cnt = count(lambda kk, k0: jnp.logical_and(kk == thr, k0 + row_iota < cand))
            return jnp.where(cnt < need, cand, p)

        p = lax.fori_loop(0, nbits, tie_body, jnp.zeros((1, tq), jnp.int32))
        p_ref[...] = jnp.where(has_thr, p, -1)

    p_last = p_ref[...]

    def write_chunk(c, carry):
        k0 = chunk_start(c)
        kk = keys_ref[pl.ds(k0, tq), :]
        sel = jnp.logical_or(kk > thr, jnp.logical_and(kk == thr, k0 + row_iota <= p_last))
        b_t = jnp.where(sel, 0.0, MASK_NEG)
        bias_ref[0, 0, c] = b_t.T.astype(BF16)
        return carry

    lax.fori_loop(0, nck, write_chunk, 0)

    def fill_chunk(c, carry):
        bias_ref[0, 0, c] = jnp.full((tq, tq), MASK_NEG, BF16)
        return carry

    lax.fori_loop(nck, n_chunks, fill_chunk, 0)


def _select(iq, ik, iw, tq, ksel):
    b, seq, _ = iq.shape
    nq = seq // tq
    return pl.pallas_call(
        functools.partial(_select_kernel, tq=tq, seq=seq, ksel=ksel),
        grid=(b, nq),
        in_specs=[pl.BlockSpec((1, tq, IDX_HEADS * LANES), lambda bi, qi: (bi, qi, 0)),
                  pl.BlockSpec((1, seq, LANES), lambda bi, qi: (bi, 0, 0)),
                  pl.BlockSpec((1, tq, LANES), lambda bi, qi: (bi, qi, 0))],
        out_specs=pl.BlockSpec((1, 1, nq, tq, tq), lambda bi, qi: (bi, qi, 0, 0, 0)),
        out_shape=jax.ShapeDtypeStruct((b, nq, nq, tq, tq), BF16),
        scratch_shapes=[pltpu.VMEM((seq, tq), jnp.int32), pltpu.VMEM((1, tq), jnp.int32)],
        compiler_params=_cparams("parallel", "parallel"),
        name="select",
    )(iq, ik, iw)


def _attn_kernel(q_ref, k_ref, v_ref, bias_ref, o_ref, qs_ref, m_ref, l_ref, acc_ref, *, tq):
    qi = pl.program_id(1)
    rows = HEADS_PER_KV * tq
    for g in range(N_KV_HEADS):
        for r in range(HEADS_PER_KV):
            hd = g * HEADS_PER_KV + r
            qs_ref[g, r * tq:(r + 1) * tq, :] = q_ref[0, :, hd * LANES:(hd + 1) * LANES]
    m_ref[...] = jnp.full(m_ref.shape, MASK_NEG, F32)
    l_ref[...] = jnp.zeros(l_ref.shape, F32)
    acc_ref[...] = jnp.zeros(acc_ref.shape, F32)

    def chunk(c, carry):
        k0 = pl.multiple_of(c * tq, tq)
        bias = bias_ref[0, 0, c].astype(F32)
        for g in range(N_KV_HEADS):
            kc = k_ref[0, pl.ds(k0, tq), g * LANES:(g + 1) * LANES]
            vc = v_ref[0, pl.ds(k0, tq), g * LANES:(g + 1) * LANES]
            lg = lax.dot_general(qs_ref[g], kc, NT_DIMS, preferred_element_type=F32)
            lg = (lg.reshape(HEADS_PER_KV, tq, tq) + bias[None]).reshape(rows, tq)
            m_old = m_ref[g]
            m_new = jnp.maximum(m_old, jnp.max(lg, axis=-1, keepdims=True))
            p = jnp.exp(lg - m_new)
            alpha = jnp.exp(m_old - m_new)
            l_ref[g] = alpha * l_ref[g] + jnp.sum(p, axis=-1, keepdims=True)
            acc_ref[g] = alpha * acc_ref[g] + jnp.dot(p.astype(BF16), vc, preferred_element_type=F32)
            m_ref[g] = m_new
        return carry

    lax.fori_loop(0, qi + 1, chunk, 0)

    for g in range(N_KV_HEADS):
        out = acc_ref[g] / l_ref[g]
        for r in range(HEADS_PER_KV):
            hd = g * HEADS_PER_KV + r
            o_ref[0, :, hd * LANES:(hd + 1) * LANES] = out[r * tq:(r + 1) * tq].astype(BF16)


def _attention(q, k, v, bias, tq):
    b, seq, _ = q.shape
    nq = seq // tq
    rows = HEADS_PER_KV * tq
    return pl.pallas_call(
        functools.partial(_attn_kernel, tq=tq),
        grid=(b, nq),
        in_specs=[pl.BlockSpec((1, tq, Q_COLS), lambda bi, qi: (bi, qi, 0)),
                  pl.BlockSpec((1, seq, KV_COLS), lambda bi, qi: (bi, 0, 0)),
                  pl.BlockSpec((1, seq, KV_COLS), lambda bi, qi: (bi, 0, 0)),
                  pl.BlockSpec((1, 1, nq, tq, tq), lambda bi, qi: (bi, qi, 0, 0, 0))],
        out_specs=pl.BlockSpec((1, tq, Q_COLS), lambda bi, qi: (bi, qi, 0)),
        out_shape=jax.ShapeDtypeStruct((b, seq, Q_COLS), BF16),
        scratch_shapes=[pltpu.VMEM((N_KV_HEADS, rows, LANES), BF16),
                        pltpu.VMEM((N_KV_HEADS, rows, 1), F32),
                        pltpu.VMEM((N_KV_HEADS, rows, 1), F32),
                        pltpu.VMEM((N_KV_HEADS, rows, LANES), F32)],
        compiler_params=_cparams("parallel", "parallel"),
        name="attn",
    )(q, k, v, bias)


CONF_HALO = 32
SC_HALO = 16
CONV_ROWS = 64


def _mix_kernel(x_ref, a_ref, ah_ref, ccx_ref, ch_ref, cb_ref, g_ref, o_ref,
                cw_ref, cbias_ref, lng_ref, lnb_ref, wconf_ref, scw_ref, wsc_ref, wattn_ref, wo_ref,
                out_ref, xa_ref, xc_ref, ya_ref, yc_ref, *, tm):
    first = pl.program_id(1) == 0
    ah = ah_ref[0].astype(F32)
    xa_ref[0:CONF_HALO, :] = jnp.where(first, 0.0, ah)
    xa_ref[CONF_HALO:, :] = a_ref[0].astype(F32)
    ch = ch_ref[0].astype(F32)
    xc_ref[0:SC_HALO, :] = jnp.where(first, 0.0, ch)
    xc_ref[SC_HALO:, :] = ccx_ref[0].astype(F32)

    def conv_lanes(lc, carry):
        ls = pl.ds(pl.multiple_of(lc * LANES, LANES), LANES)
        for rc in range(tm // CONV_ROWS):
            r0 = rc * CONV_ROWS
            acc = jnp.zeros((CONV_ROWS, LANES), F32)
            for j in range(CONF_KERNEL):
                off = CONF_HALO - (CONF_KERNEL - 1) + j
                acc = acc + xa_ref[r0 + off:r0 + off + CONV_ROWS, ls] * cw_ref[j:j + 1, ls]
            ya_ref[r0:r0 + CONV_ROWS, ls] = acc + cbias_ref[:, ls]
            acc = jnp.zeros((CONV_ROWS, LANES), F32)
            for j in range(SC_KERNEL):
                off = SC_HALO - (SC_KERNEL - 1) + j
                acc = acc + xc_ref[r0 + off:r0 + off + CONV_ROWS, ls] * scw_ref[j:j + 1, ls]
            yc_ref[r0:r0 + CONV_ROWS, ls] = acc
        return carry

    lax.fori_loop(0, D_MODEL // LANES, conv_lanes, 0)

    ya = ya_ref[...]
    mu = jnp.mean(ya, axis=-1, keepdims=True)
    yc0 = ya - mu
    rs = lax.rsqrt(jnp.mean(yc0 * yc0, axis=-1, keepdims=True) + NORM_EPS)
    ln = yc0 * rs * lng_ref[...] + lnb_ref[...]
    act = (ln * jax.nn.sigmoid(ln)).astype(BF16)
    y_conf = jnp.dot(act, wconf_ref[...], preferred_element_type=F32)
    y_sc = jnp.dot((cb_ref[0].astype(F32) * yc_ref[...]).astype(BF16), wsc_ref[...], preferred_element_type=F32)
    y_attn = jnp.dot(o_ref[0], wattn_ref[...], preferred_element_type=F32)
    g = g_ref[0]
    merged = (g[:, 0:D_MODEL].astype(F32) * y_conf + g[:, D_MODEL:2 * D_MODEL].astype(F32) * y_attn
              + g[:, 2 * D_MODEL:3 * D_MODEL].astype(F32) * y_sc)
    out_ref[0] = x_ref[0] + jnp.dot(merged.astype(BF16), wo_ref[...], preferred_element_type=F32)


def _mix(x3, a_act, ccx, cb, gates, o, conf_w, conf_b, ln_g, ln_b, w_conf, sc_w, w_sc, w_attn, w_o, tm):
    b, seq, _ = x3.shape
    nt = seq // tm

    def tile(n):
        return pl.BlockSpec((1, tm, n), lambda bi, ti: (bi, ti, 0))

    def halo(rows):
        per = tm // rows
        return pl.BlockSpec((1, rows, D_MODEL), lambda bi, ti: (bi, jnp.maximum(ti * per - 1, 0), 0))

    def const(shape):
        return pl.BlockSpec(shape, lambda bi, ti: (0,) * len(shape))

    sq = (D_MODEL, D_MODEL)
    return pl.pallas_call(
        functools.partial(_mix_kernel, tm=tm),
        grid=(b, nt),
        in_specs=[tile(D_MODEL), tile(D_MODEL), halo(CONF_HALO), tile(D_MODEL), halo(SC_HALO), tile(D_MODEL),
                  tile(3 * D_MODEL), tile(Q_COLS),
                  const((CONF_KERNEL, D_MODEL)), const((1, D_MODEL)), const((1, D_MODEL)), const((1, D_MODEL)),
                  const(sq), const((SC_KERNEL, D_MODEL)), const(sq), const(sq), const(sq)],
        out_specs=tile(D_MODEL),
        out_shape=jax.ShapeDtypeStruct((b, seq, D_MODEL), F32),
        scratch_shapes=[pltpu.VMEM((CONF_HALO + tm, D_MODEL), F32), pltpu.VMEM((SC_HALO + tm, D_MODEL), F32),
                        pltpu.VMEM((tm, D_MODEL), F32), pltpu.VMEM((tm, D_MODEL), F32)],
        compiler_params=_cparams("parallel", "parallel"),
        name="mix",
    )(x3, a_act, a_act, ccx, ccx, cb, gates, o, conf_w, conf_b, ln_g, ln_b, w_conf, sc_w, w_sc, w_attn, w_o)


FFN_CHUNK = 256
FFN_CARRY = 8


def _ffn_kernel(x_ref, g2_ref, wg_ref, wv_ref, cwg_ref, cwv_ref, cbg_ref, cbv_ref, wd_ref, out_ref,
                h_ref, ug_ref, uv_ref, carry_g, carry_v, *, tm, tiles_per_seq):
    mi = pl.program_id(0)
    f = pl.program_id(1)
    first = (mi % tiles_per_seq) == 0

    @pl.when(f == 0)
    def _():
        x = x_ref[...]
        h_ref[...] = _rms_rows(x, g2_ref[...]).astype(BF16)
        out_ref[...] = x

    def conv_branch(w_ref, u_ref, carry_ref, cw_ref, cb_ref):
        u = jnp.dot(h_ref[...], w_ref[...], preferred_element_type=F32)
        u_ref[0:FFN_CARRY, :] = jnp.where(first, 0.0, carry_ref[f])
        u_ref[FFN_CARRY:, :] = u
        carry_ref[f] = u[tm - FFN_CARRY:, :]
        out = cb_ref[...] + u * cw_ref[FFN_KERNEL - 1:FFN_KERNEL, :]
        for j in range(FFN_KERNEL - 1):
            off = FFN_CARRY - (FFN_KERNEL - 1) + j
            out = out + u_ref[pl.ds(off, tm), :] * cw_ref[j:j + 1, :]
        return out

    ug = conv_branch(wg_ref, ug_ref, carry_g, cwg_ref, cbg_ref)
    uv = conv_branch(wv_ref, uv_ref, carry_v, cwv_ref, cbv_ref)
    act = (ug * jax.nn.sigmoid(ug) * uv).astype(BF16)
    out_ref[...] += jnp.dot(act, wd_ref[...], preferred_element_type=F32)


def _ffn(x2, g2, w_up_g, w_up_v, cw_g, cw_v, cb_g, cb_v, w_down, seq, tm):
    m = x2.shape[0]
    nf = FFN_DIM // FFN_CHUNK
    return pl.pallas_call(
        functools.partial(_ffn_kernel, tm=tm, tiles_per_seq=seq // tm),
        grid=(m // tm, nf),
        in_specs=[pl.BlockSpec((tm, D_MODEL), lambda mi, f: (mi, 0)),
                  pl.BlockSpec((1, D_MODEL), lambda mi, f: (0, 0)),
                  pl.BlockSpec((D_MODEL, FFN_CHUNK), lambda mi, f: (0, f)),
                  pl.BlockSpec((D_MODEL, FFN_CHUNK), lambda mi, f: (0, f)),
                  pl.BlockSpec((FFN_KERNEL, FFN_CHUNK), lambda mi, f: (0, f)),
                  pl.BlockSpec((FFN_KERNEL, FFN_CHUNK), lambda mi, f: (0, f)),
                  pl.BlockSpec((1, FFN_CHUNK), lambda mi, f: (0, f)),
                  pl.BlockSpec((1, FFN_CHUNK), lambda mi, f: (0, f)),
                  pl.BlockSpec((FFN_CHUNK, D_MODEL), lambda mi, f: (f, 0))],
        out_specs=pl.BlockSpec((tm, D_MODEL), lambda mi, f: (mi, 0)),
        out_shape=jax.ShapeDtypeStruct((m, D_MODEL), F32),
        scratch_shapes=[pltpu.VMEM((tm, D_MODEL), BF16),
                        pltpu.VMEM((FFN_CARRY + tm, FFN_CHUNK), F32), pltpu.VMEM((FFN_CARRY + tm, FFN_CHUNK), F32),
                        pltpu.VMEM((nf, FFN_CARRY, FFN_CHUNK), F32), pltpu.VMEM((nf, FFN_CARRY, FFN_CHUNK), F32)],
        compiler_params=_cparams("arbitrary", "arbitrary"),
        name="ffn",
    )(x2, g2, w_up_g, w_up_v, cw_g, cw_v, cb_g, cb_v, w_down)


def _rope_lane_tables(seq, rot_dim, period):
    half = rot_dim // 2
    pos = jnp.arange(seq, dtype=F32)
    inv_freq = jnp.power(ROPE_THETA, -jnp.arange(0, rot_dim, 2, dtype=F32) / rot_dim)
    ang = pos[:, None] * inv_freq[None, :]
    cos, sin = jnp.cos(ang), jnp.sin(ang)
    pad = period - rot_dim
    one = jnp.ones((seq, pad), F32)
    zero = jnp.zeros((seq, pad), F32)
    zh = jnp.zeros((seq, half), F32)
    c = jnp.concatenate([cos, cos, one], axis=1)
    s1 = jnp.concatenate([zh, sin, zero], axis=1)
    s2 = jnp.concatenate([-sin, zh, zero], axis=1)
    reps = LANES // period
    return tuple(jnp.tile(t, (1, reps)) for t in (c, s1, s2))


def _pad_cols(w, n):
    return jnp.pad(w, ((0, 0), (0, 0), (0, n - w.shape[-1])))


def _tile_rows(seq, target):
    t = min(target, seq)
    assert seq % t == 0
    return t


def kernel(x, norm1_g, w_in, q_norm_g, k_norm_g, w_attn_out, conf_conv_w, conf_conv_b, conf_ln_g, conf_ln_b,
           w_conf_out, sc_conv_w, w_sc_out, w_o, norm2_g, w_up, ffn_conv_w, ffn_conv_b, w_down):
    b, seq, d = x.shape
    depth = w_in.shape[0]
    assert d == D_MODEL and seq % LANES == 0
    tq = _tile_rows(seq, 256)
    tm_proj = _tile_rows(seq, 512)
    tm_mix = _tile_rows(seq, 256)
    tm_ffn = _tile_rows(seq, 512)
    ksel = min(TOPK_MAX, seq // 4)

    tabs_a = _rope_lane_tables(seq, HEAD_DIM // ROPE_FRACTION_DIV, LANES)
    tabs_i = _rope_lane_tables(seq, IDX_DIM // ROPE_FRACTION_DIV, LANES)

    offs = [0]
    for s in IN_SIZES:
        offs.append(offs[-1] + s)
    wq, wk, wv, wiq, wik, wiw, wa, wc, wg = (w_in[:, :, offs[i]:offs[i + 1]] for i in range(len(IN_SIZES)))
    wiq = jnp.pad(wiq.reshape(depth, d, IDX_HEADS, IDX_DIM), ((0, 0), (0, 0), (0, 0), (0, LANES - IDX_DIM)))
    wiq = wiq.reshape(depth, d, IDX_HEADS * LANES)
    w_att = jnp.concatenate([wq, wk, wv, wiq, _pad_cols(wik, LANES), _pad_cols(wiw, LANES)], axis=-1).astype(BF16)
    layers = dict(
        g1=norm1_g[:, None, :], w_att=w_att, w_a=wa.astype(BF16), w_c=wc.astype(BF16), w_g=wg.astype(BF16),
        qg=q_norm_g[:, None, :], kg=k_norm_g[:, None, :],
        conf_w=conf_conv_w, conf_b=conf_conv_b[:, None, :], ln_g=conf_ln_g[:, None, :], ln_b=conf_ln_b[:, None, :],
        w_conf=w_conf_out.astype(BF16), sc_w=sc_conv_w, w_sc=w_sc_out.astype(BF16),
        w_attn=w_attn_out.astype(BF16), w_o=w_o.astype(BF16),
        g2=norm2_g[:, None, :],
        w_up_g=w_up[:, :, :FFN_DIM].astype(BF16), w_up_v=w_up[:, :, FFN_DIM:].astype(BF16),
        cw_g=ffn_conv_w[:, :, :FFN_DIM], cw_v=ffn_conv_w[:, :, FFN_DIM:],
        cb_g=ffn_conv_b[:, None, :FFN_DIM], cb_v=ffn_conv_b[:, None, FFN_DIM:],
        w_down=w_down.astype(BF16),
    )

    def layer(xc, p):
        x2 = xc.reshape(b * seq, d)
        q, k, v, iq, ik, iw, a_act, cb, ccx, gates = _input_projections(
            x2, p["g1"], p["w_att"], p["w_a"], p["w_c"], p["w_g"], p["qg"], p["kg"], tabs_a, tabs_i, seq, tm_proj)
        r3 = lambda t: t.reshape(b, seq, t.shape[-1])
        bias = _select(r3(iq), r3(ik), r3(iw), tq, ksel)
        o = _attention(r3(q), r3(k), r3(v), bias, tq)
        x1 = _mix(xc, r3(a_act), r3(ccx), r3(cb), r3(gates), o, p["conf_w"], p["conf_b"], p["ln_g"], p["ln_b"],
                  p["w_conf"], p["sc_w"], p["w_sc"], p["w_attn"], p["w_o"], tm_mix)
        x_out = _ffn(x1.reshape(b * seq, d), p["g2"], p["w_up_g"], p["w_up_v"], p["cw_g"], p["cw_v"],
                     p["cb_g"], p["cb_v"], p["w_down"], seq, tm_ffn)
        return x_out.reshape(b, seq, d), None

    out, _ = lax.scan(layer, x, layers)
    return out
```

```python
import functools

import jax
import jax.numpy as jnp
from jax import lax
from jax.experimental import pallas as pl
from jax.experimental.pallas import tpu as pltpu

D_MODEL = 1024
N_HEADS = 8
HEAD_DIM = 128
N_KV_HEADS = 2
HEADS_PER_KV = N_HEADS // N_KV_HEADS
ROPE_FRACTION_DIV = 4
ROPE_THETA = 500000.0
IDX_HEADS = 8
IDX_DIM = 64
TOPK_MAX = 256
CONF_KERNEL = 31
SC_KERNEL = 3
FFN_DIM = 2816
FFN_KERNEL = 3
NORM_EPS = 1e-6

Q_COLS = N_HEADS * HEAD_DIM
KV_COLS = N_KV_HEADS * HEAD_DIM
IQ_COLS = IDX_HEADS * IDX_DIM
IN_SIZES = (Q_COLS, KV_COLS, KV_COLS, IQ_COLS, IDX_DIM, IDX_HEADS, 2 * D_MODEL, 3 * D_MODEL, 3 * D_MODEL)

LANES = 128
VMEM_LIMIT_BYTES = 56 * 1024 * 1024
MASK_NEG = -1e30
INT_MIN = -(2 ** 31)
LOG2_E = 1.4426950408889634

F32 = jnp.float32
BF16 = jnp.bfloat16
NT_DIMS = (((1,), (1,)), ((), ()))


def _cparams(*sem):
    return pltpu.CompilerParams(dimension_semantics=sem, vmem_limit_bytes=VMEM_LIMIT_BYTES)


def _rms_rows(x, g):
    ms = jnp.mean(x * x, axis=-1, keepdims=True)
    return x * lax.rsqrt(ms + NORM_EPS) * g


FOLD_WAYS = 8


def _fold_rows(x, op):
    rows, n = x.shape
    part = op(x.reshape(FOLD_WAYS, rows // (8 * FOLD_WAYS), 8, n), axis=1)
    return op(part, axis=0)


def _rope(y, c, s1, s2, half):
    return y * c + pltpu.roll(y, half, 1) * s1 + pltpu.roll(y, LANES - half, 1) * s2


def _proj_attn_kernel(x_ref, g1_ref, w_ref, wvt_ref, qg_ref, kg_ref, ca_ref, s1a_ref, s2a_ref,
                      ci_ref, s1i_ref, s2i_ref,
                      q_out, k_out, vt_out, iq_out, ik_out, iw_out):
    h = _rms_rows(x_ref[...], g1_ref[...]).astype(BF16)
    ca, s1a, s2a = ca_ref[...], s1a_ref[...], s2a_ref[...]
    ci, s1i, s2i = ci_ref[...], s1i_ref[...], s2i_ref[...]
    half_a = HEAD_DIM // ROPE_FRACTION_DIV // 2
    half_i = IDX_DIM // ROPE_FRACTION_DIV // 2

    def cols(j, n=1):
        return jnp.dot(h, w_ref[:, j * LANES:(j + n) * LANES], preferred_element_type=F32)

    qscale = HEAD_DIM ** -0.5 * LOG2_E
    for hd in range(N_HEADS):
        y = _rope(_rms_rows(cols(hd), qg_ref[...]), ca, s1a, s2a, half_a)
        q_out[:, hd * LANES:(hd + 1) * LANES] = (y * qscale).astype(BF16)
    base = N_HEADS
    for hd in range(N_KV_HEADS):
        y = _rope(_rms_rows(cols(base + hd), kg_ref[...]), ca, s1a, s2a, half_a)
        k_out[:, hd * LANES:(hd + 1) * LANES] = y.astype(BF16)
    base += N_KV_HEADS
    v_t = lax.dot_general(wvt_ref[...], h, NT_DIMS, preferred_element_type=F32)
    tk = vt_out.shape[-1]
    for j in range(vt_out.shape[0]):
        vt_out[j] = v_t[:, j * tk:(j + 1) * tk].astype(BF16)
    for hd in range(IDX_HEADS):
        y = _rope(cols(base + hd), ci, s1i, s2i, half_i)
        iq_out[:, hd * LANES:(hd + 1) * LANES] = y.astype(BF16)
    base += IDX_HEADS
    ik_out[...] = _rope(cols(base), ci, s1i, s2i, half_i).astype(BF16)
    iw_out[...] = cols(base + 1) * (IDX_HEADS ** -0.5 * IDX_DIM ** -0.5)


def _proj_a_kernel(x_ref, g1_ref, w_ref, a_out):
    h = _rms_rows(x_ref[...], g1_ref[...]).astype(BF16)
    cw = 2 * LANES
    for j in range(D_MODEL // cw):
        val = jnp.dot(h, w_ref[:, j * cw:(j + 1) * cw], preferred_element_type=F32)
        gate = jnp.dot(h, w_ref[:, D_MODEL + j * cw:D_MODEL + (j + 1) * cw], preferred_element_type=F32)
        a_out[:, j * cw:(j + 1) * cw] = (val * jax.nn.sigmoid(gate)).astype(BF16)


def _proj_c_kernel(x_ref, g1_ref, w_ref, cb_out, ccx_out):
    h = _rms_rows(x_ref[...], g1_ref[...]).astype(BF16)
    cw = 2 * LANES
    for j in range(D_MODEL // cw):
        cb = jnp.dot(h, w_ref[:, j * cw:(j + 1) * cw], preferred_element_type=F32)
        cc = jnp.dot(h, w_ref[:, D_MODEL + j * cw:D_MODEL + (j + 1) * cw], preferred_element_type=F32)
        cx = jnp.dot(h, w_ref[:, 2 * D_MODEL + j * cw:2 * D_MODEL + (j + 1) * cw], preferred_element_type=F32)
        cb_out[:, j * cw:(j + 1) * cw] = cb.astype(BF16)
        ccx_out[:, j * cw:(j + 1) * cw] = (cc * cx).astype(BF16)


def _proj_g_kernel(x_ref, g1_ref, w_ref, g_out):
    h = _rms_rows(x_ref[...], g1_ref[...]).astype(BF16)
    cw = 2 * LANES
    for j in range(3 * D_MODEL // cw):
        g = jnp.dot(h, w_ref[:, j * cw:(j + 1) * cw], preferred_element_type=F32)
        g_out[:, j * cw:(j + 1) * cw] = jax.nn.sigmoid(g).astype(BF16)


def _row_spec(tm, n):
    return pl.BlockSpec((tm, n), lambda i: (i, 0))


def _const_spec(shape):
    return pl.BlockSpec(shape, lambda i: (0,) * len(shape))


def _input_projections(x2, g1, w_att, w_vt, w_a, w_c, w_g, qg, kg, tabs_a, tabs_i, seq, tm, tk):
    m = x2.shape[0]
    grid = (m // tm,)
    tpb = seq // tm
    tab_spec = pl.BlockSpec((tm, LANES), lambda i: (i % tpb, 0))
    n_att = w_att.shape[1]
    q, k, vt, iq, ik, iw = pl.pallas_call(
        _proj_attn_kernel,
        grid=grid,
        in_specs=[_row_spec(tm, D_MODEL), _const_spec((1, D_MODEL)), _const_spec((D_MODEL, n_att)),
                  _const_spec((KV_COLS, D_MODEL)), _const_spec((1, LANES)), _const_spec((1, LANES))]
        + [tab_spec] * 6,
        out_specs=[_row_spec(tm, Q_COLS), _row_spec(tm, KV_COLS),
                   pl.BlockSpec((tm // tk, KV_COLS, tk), lambda i: (i, 0, 0)),
                   _row_spec(tm, IDX_HEADS * LANES), _row_spec(tm, LANES), _row_spec(tm, LANES)],
        out_shape=[jax.ShapeDtypeStruct((m, Q_COLS), BF16), jax.ShapeDtypeStruct((m, KV_COLS), BF16),
                   jax.ShapeDtypeStruct((m // tk, KV_COLS, tk), BF16),
                   jax.ShapeDtypeStruct((m, IDX_HEADS * LANES), BF16),
                   jax.ShapeDtypeStruct((m, LANES), BF16), jax.ShapeDtypeStruct((m, LANES), F32)],
        compiler_params=_cparams("parallel"),
        name="proj_attn",
    )(x2, g1, w_att, w_vt, qg, kg, *tabs_a, *tabs_i)
    a_act = pl.pallas_call(
        _proj_a_kernel, grid=grid,
        in_specs=[_row_spec(tm, D_MODEL), _const_spec((1, D_MODEL)), _const_spec((D_MODEL, 2 * D_MODEL))],
        out_specs=_row_spec(tm, D_MODEL),
        out_shape=jax.ShapeDtypeStruct((m, D_MODEL), BF16),
        compiler_params=_cparams("parallel"), name="proj_a",
    )(x2, g1, w_a)
    cb, ccx = pl.pallas_call(
        _proj_c_kernel, grid=grid,
        in_specs=[_row_spec(tm, D_MODEL), _const_spec((1, D_MODEL)), _const_spec((D_MODEL, 3 * D_MODEL))],
        out_specs=[_row_spec(tm, D_MODEL), _row_spec(tm, D_MODEL)],
        out_shape=[jax.ShapeDtypeStruct((m, D_MODEL), BF16)] * 2,
        compiler_params=_cparams("parallel"), name="proj_c",
    )(x2, g1, w_c)
    gates = pl.pallas_call(
        _proj_g_kernel, grid=grid,
        in_specs=[_row_spec(tm, D_MODEL), _const_spec((1, D_MODEL)), _const_spec((D_MODEL, 3 * D_MODEL))],
        out_specs=_row_spec(tm, 3 * D_MODEL),
        out_shape=jax.ShapeDtypeStruct((m, 3 * D_MODEL), BF16),
        compiler_params=_cparams("parallel"), name="proj_g",
    )(x2, g1, w_g)
    return q, k, vt, iq, ik, iw, a_act, cb, ccx, gates


def _dsa_kernel(iq_ref, ik_ref, iw_ref, q_ref, k_ref, vt_ref, o_ref, keys_ref, p_ref, acc_ref,
                bias_ref, lg_ref, pb_ref, *, tq, tk, seq, ksel):
    qi = pl.program_id(1)
    n_it = ((qi + 1) * tq + tk - 1) // tk
    iw_t = iw_ref[0].T
    t_idx = qi * tq + lax.broadcasted_iota(jnp.int32, (tk, tq), 1)
    row_iota = lax.broadcasted_iota(jnp.int32, (tk, tq), 0)

    def chunk_start(c):
        return pl.multiple_of(c * tk, tk)

    def score_chunk(c, carry):
        k0 = chunk_start(c)
        ikc = ik_ref[0, pl.ds(k0, tk), :]
        acc = jnp.zeros((tk, tq), F32)
        for hd in range(IDX_HEADS):
            iqh = iq_ref[0, :, hd * LANES:(hd + 1) * LANES]
            rel = lax.dot_general(ikc, iqh, NT_DIMS, preferred_element_type=F32)
            acc = acc + jnp.maximum(rel, 0.0) * iw_t[hd:hd + 1, :]
        bits = lax.bitcast_convert_type(acc, jnp.int32)
        key = jnp.where(bits < 0, -(bits & 0x7FFFFFFF), bits)
        key = jnp.where(k0 + row_iota <= t_idx, key, INT_MIN)
        keys_ref[pl.ds(k0, tk), :] = key
        return carry

    lax.fori_loop(0, n_it, score_chunk, 0)

    def count(pred):
        def body(c, acc8):
            k0 = chunk_start(c)
            m = jnp.where(pred(keys_ref[pl.ds(k0, tk), :], k0), 1.0, 0.0)
            return acc8 + _fold_rows(m, jnp.sum)
        acc8 = lax.fori_loop(0, n_it, body, jnp.zeros((8, tq), F32))
        return acc8.sum(axis=0, keepdims=True)

    kf = float(ksel)

    def bit_body(i, t_u):
        cand_u = t_u | jnp.left_shift(jnp.int32(1), 31 - i)
        cand_s = cand_u ^ INT_MIN
        cnt = count(lambda kk, k0: kk >= cand_s)
        return jnp.where(cnt >= kf, cand_u, t_u)

    thr = lax.fori_loop(0, 32, bit_body, jnp.zeros((1, tq), jnp.int32)) ^ INT_MIN

    c_gt = count(lambda kk, k0: kk > thr)
    c_eq = count(lambda kk, k0: kk == thr)
    need = kf - c_gt
    has_thr = thr != INT_MIN
    excess = jnp.logical_and(c_eq > need, has_thr)

    p_ref[...] = jnp.where(has_thr, seq, -1).astype(jnp.int32)

    @pl.when(jnp.max(jnp.where(excess, 1.0, 0.0)) > 0.0)
    def _():
        nbits = (seq - 1).bit_length()

        def tie_body(i, p):
            cand = p | jnp.left_shift(jnp.int32(1), nbits - 1 - i)
            cnt = count(lambda kk, k0: jnp.logical_and(kk == thr, k0 + row_iota < cand))
            return jnp.where(cnt < need, cand, p)

        p = lax.fori_loop(0, nbits, tie_body, jnp.zeros((1, tq), jnp.int32))
        p_ref[...] = jnp.where(has_thr, p, -1)

    p_last = p_ref[...]

    acc_ref[...] = jnp.zeros(acc_ref.shape, F32)

    def attend(c, stats):
        k0 = chunk_start(c)
        kk = keys_ref[pl.ds(k0, tk), :]
        sel = jnp.logical_or(kk > thr, jnp.logical_and(kk == thr, k0 + row_iota <= p_last))
        bias_ref[...] = jnp.where(sel, 0.0, MASK_NEG)
        new_stats = []

        def logits(hd):
            g = hd // HEADS_PER_KV
            kc = k_ref[0, pl.ds(k0, tk), g * LANES:(g + 1) * LANES]
            qh = q_ref[0, :, hd * LANES:(hd + 1) * LANES]
            lg_ref[hd % 2] = lax.dot_general(kc, qh, NT_DIMS, preferred_element_type=F32) + bias_ref[...]

        def softmax_pv(hd):
            g = hd // HEADS_PER_KV
            m_old, l_old = stats[hd]
            cmax = _fold_rows(lg_ref[hd % 2], jnp.max).max(axis=0, keepdims=True)
            m_new = jnp.maximum(m_old, cmax)
            p = jnp.exp2(lg_ref[hd % 2] - m_new)
            pb_ref[hd % 2] = p.astype(BF16)
            alpha = jnp.exp2(m_old - m_new)
            l_new = alpha * l_old + _fold_rows(p, jnp.sum).sum(axis=0, keepdims=True)
            vtc = vt_ref[0, c, g * LANES:(g + 1) * LANES, :]
            pv = jnp.dot(vtc, pb_ref[hd % 2], preferred_element_type=F32)
            new_stats.append((m_new, l_new))
            return alpha, pv

        pvs = {}
        for s in range(N_HEADS + 2):
            if s < N_HEADS:
                logits(s)
            if 1 <= s <= N_HEADS:
                pvs[s - 1] = softmax_pv(s - 1)
            if s >= 2:
                alpha, pv = pvs.pop(s - 2)
                acc_ref[s - 2] = alpha * acc_ref[s - 2] + pv
        return tuple(new_stats)

    init = tuple((jnp.full((1, tq), MASK_NEG, F32), jnp.zeros((1, tq), F32)) for _ in range(N_HEADS))
    stats = lax.fori_loop(0, n_it, attend, init)

    for hd in range(N_HEADS):
        out_t = acc_ref[hd] / stats[hd][1]
        o_ref[0, :, hd * LANES:(hd + 1) * LANES] = out_t.T.astype(BF16)


def _sparse_attention(iq, ik, iw, q, k, vt, tq, tk, ksel):
    b, seq, _ = iq.shape
    nq = seq // tq
    nc = seq // tk
    tile = lambda n: pl.BlockSpec((1, tq, n), lambda bi, qi: (bi, qi, 0))
    whole = lambda n: pl.BlockSpec((1, seq, n), lambda bi, qi: (bi, 0, 0))
    return pl.pallas_call(
        functools.partial(_dsa_kernel, tq=tq, tk=tk, seq=seq, ksel=ksel),
        grid=(b, nq),
        in_specs=[tile(IDX_HEADS * LANES), whole(LANES), tile(LANES), tile(Q_COLS), whole(KV_COLS),
                  pl.BlockSpec((1, nc, KV_COLS, tk), lambda bi, qi: (bi, 0, 0, 0))],
        out_specs=tile(Q_COLS),
        out_shape=jax.ShapeDtypeStruct((b, seq, Q_COLS), BF16),
        scratch_shapes=[pltpu.VMEM((seq, tq), jnp.int32), pltpu.VMEM((1, tq), jnp.int32),
                        pltpu.VMEM((N_HEADS, LANES, tq), F32), pltpu.VMEM((tk, tq), F32),
                        pltpu.VMEM((2, tk, tq), F32), pltpu.VMEM((2, tk, tq), BF16)],
        compiler_params=_cparams("parallel", "parallel"),
        name="dsa",
    )(iq, ik, iw, q, k, vt)


CONF_HALO = 32
SC_HALO = 16
CONV_ROWS = 64


def _mix_kernel(x_ref, a_ref, ah_ref, ccx_ref, ch_ref, cb_ref, g_ref, o_ref,
                cw_ref, cbias_ref, lng_ref, lnb_ref, wconf_ref, scw_ref, wsc_ref, wattn_ref, wo_ref,
                out_ref, xa_ref, xc_ref, ya_ref, yc_ref, *, tm):
    first = pl.program_id(1) == 0
    ah = ah_ref[0].astype(F32)
    xa_ref[0:CONF_HALO, :] = jnp.where(first, 0.0, ah)
    xa_ref[CONF_HALO:, :] = a_ref[0].astype(F32)
    ch = ch_ref[0].astype(F32)
    xc_ref[0:SC_HALO, :] = jnp.where(first, 0.0, ch)
    xc_ref[SC_HALO:, :] = ccx_ref[0].astype(F32)

    def conv_lanes(lc, carry):
        ls = pl.ds(pl.multiple_of(lc * LANES, LANES), LANES)
        for rc in range(tm // CONV_ROWS):
            r0 = rc * CONV_ROWS
            acc = jnp.zeros((CONV_ROWS, LANES), F32)
            for j in range(CONF_KERNEL):
                off = CONF_HALO - (CONF_KERNEL - 1) + j
                acc = acc + xa_ref[r0 + off:r0 + off + CONV_ROWS, ls] * cw_ref[j:j + 1, ls]
            ya_ref[r0:r0 + CONV_ROWS, ls] = acc + cbias_ref[:, ls]
            acc = jnp.zeros((CONV_ROWS, LANES), F32)
            for j in range(SC_KERNEL):
                off = SC_HALO - (SC_KERNEL - 1) + j
                acc = acc + xc_ref[r0 + off:r0 + off + CONV_ROWS, ls] * scw_ref[j:j + 1, ls]
            yc_ref[r0:r0 + CONV_ROWS, ls] = acc
        return carry

    lax.fori_loop(0, D_MODEL // LANES, conv_lanes, 0)

    ya = ya_ref[...]
    mu = jnp.mean(ya, axis=-1, keepdims=True)
    yc0 = ya - mu
    rs = lax.rsqrt(jnp.mean(yc0 * yc0, axis=-1, keepdims=True) + NORM_EPS)
    ln = yc0 * rs * lng_ref[...] + lnb_ref[...]
    act = (ln * jax.nn.sigmoid(ln)).astype(BF16)
    y_conf = jnp.dot(act, wconf_ref[...], preferred_element_type=F32)
    y_sc = jnp.dot((cb_ref[0].astype(F32) * yc_ref[...]).astype(BF16), wsc_ref[...], preferred_element_type=F32)
    y_attn = jnp.dot(o_ref[0], wattn_ref[...], preferred_element_type=F32)
    g = g_ref[0]
    merged = (g[:, 0:D_MODEL].astype(F32) * y_conf + g[:, D_MODEL:2 * D_MODEL].astype(F32) * y_attn
              + g[:, 2 * D_MODEL:3 * D_MODEL].astype(F32) * y_sc)
    out_ref[0] = x_ref[0] + jnp.dot(merged.astype(BF16), wo_ref[...], preferred_element_type=F32)


def _mix(x3, a_act, ccx, cb, gates, o, conf_w, conf_b, ln_g, ln_b, w_conf, sc_w, w_sc, w_attn, w_o, tm):
    b, seq, _ = x3.shape
    nt = seq // tm

    def tile(n):
        return pl.BlockSpec((1, tm, n), lambda bi, ti: (bi, ti, 0))

    def halo(rows):
        per = tm // rows
        return pl.BlockSpec((1, rows, D_MODEL), lambda bi, ti: (bi, jnp.maximum(ti * per - 1, 0), 0))

    def const(shape):
        return pl.BlockSpec(shape, lambda bi, ti: (0,) * len(shape))

    sq = (D_MODEL, D_MODEL)
    return pl.pallas_call(
        functools.partial(_mix_kernel, tm=tm),
        grid=(b, nt),
        in_specs=[tile(D_MODEL), tile(D_MODEL), halo(CONF_HALO), tile(D_MODEL), halo(SC_HALO), tile(D_MODEL),
                  tile(3 * D_MODEL), tile(Q_COLS),
                  const((CONF_KERNEL, D_MODEL)), const((1, D_MODEL)), const((1, D_MODEL)), const((1, D_MODEL)),
                  const(sq), const((SC_KERNEL, D_MODEL)), const(sq), const(sq), const(sq)],
        out_specs=tile(D_MODEL),
        out_shape=jax.ShapeDtypeStruct((b, seq, D_MODEL), F32),
        scratch_shapes=[pltpu.VMEM((CONF_HALO + tm, D_MODEL), F32), pltpu.VMEM((SC_HALO + tm, D_MODEL), F32),
                        pltpu.VMEM((tm, D_MODEL), F32), pltpu.VMEM((tm, D_MODEL), F32)],
        compiler_params=_cparams("parallel", "parallel"),
        name="mix",
    )(x3, a_act, a_act, ccx, ccx, cb, gates, o, conf_w, conf_b, ln_g, ln_b, w_conf, sc_w, w_sc, w_attn, w_o)


FFN_CHUNK = 256
FFN_CARRY = 8


def _ffn_kernel(x_ref, g2_ref, wg_ref, wv_ref, cwg_ref, cwv_ref, cbg_ref, cbv_ref, wd_ref, out_ref,
                h_ref, ug_ref, uv_ref, carry_g, carry_v, *, tm, tiles_per_seq):
    mi = pl.program_id(0)
    f = pl.program_id(1)
    first = (mi % tiles_per_seq) == 0

    @pl.when(f == 0)
    def _():
        x = x_ref[...]
        h_ref[...] = _rms_rows(x, g2_ref[...]).astype(BF16)
        out_ref[...] = x

    def conv_branch(w_ref, u_ref, carry_ref, cw_ref, cb_ref):
        u = jnp.dot(h_ref[...], w_ref[...], preferred_element_type=F32)
        u_ref[0:FFN_CARRY, :] = jnp.where(first, 0.0, carry_ref[f])
        u_ref[FFN_CARRY:, :] = u
        carry_ref[f] = u[tm - FFN_CARRY:, :]
        out = cb_ref[...] + u * cw_ref[FFN_KERNEL - 1:FFN_KERNEL, :]
        for j in range(FFN_KERNEL - 1):
            off = FFN_CARRY - (FFN_KERNEL - 1) + j
            out = out + u_ref[pl.ds(off, tm), :] * cw_ref[j:j + 1, :]
        return out

    ug = conv_branch(wg_ref, ug_ref, carry_g, cwg_ref, cbg_ref)
    uv = conv_branch(wv_ref, uv_ref, carry_v, cwv_ref, cbv_ref)
    act = (ug * jax.nn.sigmoid(ug) * uv).astype(BF16)
    out_ref[...] += jnp.dot(act, wd_ref[...], preferred_element_type=F32)


def _ffn(x2, g2, w_up_g, w_up_v, cw_g, cw_v, cb_g, cb_v, w_down, seq, tm):
    m = x2.shape[0]
    nf = FFN_DIM // FFN_CHUNK
    return pl.pallas_call(
        functools.partial(_ffn_kernel, tm=tm, tiles_per_seq=seq // tm),
        grid=(m // tm, nf),
        in_specs=[pl.BlockSpec((tm, D_MODEL), lambda mi, f: (mi, 0)),
                  pl.BlockSpec((1, D_MODEL), lambda mi, f: (0, 0)),
                  pl.BlockSpec((D_MODEL, FFN_CHUNK), lambda mi, f: (0, f)),
                  pl.BlockSpec((D_MODEL, FFN_CHUNK), lambda mi, f: (0, f)),
                  pl.BlockSpec((FFN_KERNEL, FFN_CHUNK), lambda mi, f: (0, f)),
                  pl.BlockSpec((FFN_KERNEL, FFN_CHUNK), lambda mi, f: (0, f)),
                  pl.BlockSpec((1, FFN_CHUNK), lambda mi, f: (0, f)),
                  pl.BlockSpec((1, FFN_CHUNK), lambda mi, f: (0, f)),
                  pl.BlockSpec((FFN_CHUNK, D_MODEL), lambda mi, f: (f, 0))],
        out_specs=pl.BlockSpec((tm, D_MODEL), lambda mi, f: (mi, 0)),
        out_shape=jax.ShapeDtypeStruct((m, D_MODEL), F32),
        scratch_shapes=[pltpu.VMEM((tm, D_MODEL), BF16),
                        pltpu.VMEM((FFN_CARRY + tm, FFN_CHUNK), F32), pltpu.VMEM((FFN_CARRY + tm, FFN_CHUNK), F32),
                        pltpu.VMEM((nf, FFN_CARRY, FFN_CHUNK), F32), pltpu.VMEM((nf, FFN_CARRY, FFN_CHUNK), F32)],
        compiler_params=_cparams("arbitrary", "arbitrary"),
        name="ffn",
    )(x2, g2, w_up_g, w_up_v, cw_g, cw_v, cb_g, cb_v, w_down)


def _rope_lane_tables(seq, rot_dim, period):
    half = rot_dim // 2
    pos = jnp.arange(seq, dtype=F32)
    inv_freq = jnp.power(ROPE_THETA, -jnp.arange(0, rot_dim, 2, dtype=F32) / rot_dim)
    ang = pos[:, None] * inv_freq[None, :]
    cos, sin = jnp.cos(ang), jnp.sin(ang)
    pad = period - rot_dim
    one = jnp.ones((seq, pad), F32)
    zero = jnp.zeros((seq, pad), F32)
    zh = jnp.zeros((seq, half), F32)
    c = jnp.concatenate([cos, cos, one], axis=1)
    s1 = jnp.concatenate([zh, sin, zero], axis=1)
    s2 = jnp.concatenate([-sin, zh, zero], axis=1)
    reps = LANES // period
    return tuple(jnp.tile(t, (1, reps)) for t in (c, s1, s2))


def _pad_cols(w, n):
    return jnp.pad(w, ((0, 0), (0, 0), (0, n - w.shape[-1])))


def _tile_rows(seq, target):
    t = min(target, seq)
    assert seq % t == 0
    return t


def kernel(x, norm1_g, w_in, q_norm_g, k_norm_g, w_attn_out, conf_conv_w, conf_conv_b, conf_ln_g, conf_ln_b,
           w_conf_out, sc_conv_w, w_sc_out, w_o, norm2_g, w_up, ffn_conv_w, ffn_conv_b, w_down):
    b, seq, d = x.shape
    depth = w_in.shape[0]
    assert d == D_MODEL and seq % LANES == 0
    tq = _tile_rows(seq, 256)
    tk = _tile_rows(seq, 512)
    tm_proj = _tile_rows(seq, 512)
    tm_mix = _tile_rows(seq, 256)
    tm_ffn = _tile_rows(seq, 512)
    ksel = min(TOPK_MAX, seq // 4)

    tabs_a = _rope_lane_tables(seq, HEAD_DIM // ROPE_FRACTION_DIV, LANES)
    tabs_i = _rope_lane_tables(seq, IDX_DIM // ROPE_FRACTION_DIV, LANES)

    offs = [0]
    for s in IN_SIZES:
        offs.append(offs[-1] + s)
    wq, wk, wv, wiq, wik, wiw, wa, wc, wg = (w_in[:, :, offs[i]:offs[i + 1]] for i in range(len(IN_SIZES)))
    wiq = jnp.pad(wiq.reshape(depth, d, IDX_HEADS, IDX_DIM), ((0, 0), (0, 0), (0, 0), (0, LANES - IDX_DIM)))
    wiq = wiq.reshape(depth, d, IDX_HEADS * LANES)
    w_att = jnp.concatenate([wq, wk, wiq, _pad_cols(wik, LANES), _pad_cols(wiw, LANES)], axis=-1).astype(BF16)
    layers = dict(
        g1=norm1_g[:, None, :], w_att=w_att, w_vt=jnp.swapaxes(wv, 1, 2).astype(BF16),
        w_a=wa.astype(BF16), w_c=wc.astype(BF16), w_g=wg.astype(BF16),
        qg=q_norm_g[:, None, :], kg=k_norm_g[:, None, :],
        conf_w=conf_conv_w, conf_b=conf_conv_b[:, None, :], ln_g=conf_ln_g[:, None, :], ln_b=conf_ln_b[:, None, :],
        w_conf=w_conf_out.astype(BF16), sc_w=sc_conv_w, w_sc=w_sc_out.astype(BF16),
        w_attn=w_attn_out.astype(BF16), w_o=w_o.astype(BF16),
        g2=norm2_g[:, None, :],
        w_up_g=w_up[:, :, :FFN_DIM].astype(BF16), w_up_v=w_up[:, :, FFN_DIM:].astype(BF16),
        cw_g=ffn_conv_w[:, :, :FFN_DIM], cw_v=ffn_conv_w[:, :, FFN_DIM:],
        cb_g=ffn_conv_b[:, None, :FFN_DIM], cb_v=ffn_conv_b[:, None, FFN_DIM:],
        w_down=w_down.astype(BF16),
    )

    def layer(xc, p):
        x2 = xc.reshape(b * seq, d)
        q, k, vt, iq, ik, iw, a_act, cb, ccx, gates = _input_projections(
            x2, p["g1"], p["w_att"], p["w_vt"], p["w_a"], p["w_c"], p["w_g"], p["qg"], p["kg"],
            tabs_a, tabs_i, seq, tm_proj, tk)
        r3 = lambda t: t.reshape(b, seq, t.shape[-1])
        o = _sparse_attention(r3(iq), r3(ik), r3(iw), r3(q), r3(k), vt.reshape(b, seq // tk, KV_COLS, tk),
                              tq, tk, ksel)
        x1 = _mix(xc, r3(a_act), r3(ccx), r3(cb), r3(gates), o, p["conf_w"], p["conf_b"], p["ln_g"], p["ln_b"],
                  p["w_conf"], p["sc_w"], p["w_sc"], p["w_attn"], p["w_o"], tm_mix)
        x_out = _ffn(x1.reshape(b * seq, d), p["g2"], p["w_up_g"], p["w_up_v"], p["cw_g"], p["cw_v"],
                     p["cb_g"], p["cb_v"], p["w_down"], seq, tm_ffn)
        return x_out.reshape(b, seq, d), None

    out, _ = lax.scan(layer, x, layers)
    return out
```

```python
import functools

import jax
import jax.numpy as jnp
from jax import lax
from jax.experimental import pallas as pl
from jax.experimental.pallas import tpu as pltpu

D_MODEL = 1024
N_HEADS = 8
HEAD_DIM = 128
N_KV_HEADS = 2
HEADS_PER_KV = N_HEADS // N_KV_HEADS
ROPE_FRACTION_DIV = 4
ROPE_THETA = 500000.0
IDX_HEADS = 8
IDX_DIM = 64
TOPK_MAX = 256
CONF_KERNEL = 31
SC_KERNEL = 3
FFN_DIM = 2816
FFN_KERNEL = 3
NORM_EPS = 1e-6

Q_COLS = N_HEADS * HEAD_DIM
KV_COLS = N_KV_HEADS * HEAD_DIM
IQ_COLS = IDX_HEADS * IDX_DIM
IN_SIZES = (Q_COLS, KV_COLS, KV_COLS, IQ_COLS, IDX_DIM, IDX_HEADS, 2 * D_MODEL, 3 * D_MODEL, 3 * D_MODEL)

LANES = 128
SUBLANES = 8
VMEM_LIMIT_BYTES = 56 * 1024 * 1024
MASK_NEG = -1e30
INT_MIN = -(2 ** 31)
LOG2_E = 1.4426950408889634

F32 = jnp.float32
BF16 = jnp.bfloat16
NT_DIMS = (((1,), (1,)), ((), ()))


def _cparams(*sem):
    return pltpu.CompilerParams(dimension_semantics=sem, vmem_limit_bytes=VMEM_LIMIT_BYTES)


def _rms_rows(x, g):
    ms = jnp.mean(x * x, axis=-1, keepdims=True)
    return x * lax.rsqrt(ms + NORM_EPS) * g


FOLD_WAYS = 4
COUNT_ROWS = 64


def _fold_rows(x, op):
    rows, n = x.shape
    part = op(x.reshape(FOLD_WAYS, rows // (8 * FOLD_WAYS), 8, n), axis=1)
    return op(part, axis=0)


def _rope(y, c, s1, s2, half):
    return y * c + pltpu.roll(y, half, 1) * s1 + pltpu.roll(y, LANES - half, 1) * s2


def _proj_attn_kernel(x_ref, g1_ref, w_ref, wvt_ref, qg_ref, kg_ref, ca_ref, s1a_ref, s2a_ref,
                      ci_ref, s1i_ref, s2i_ref,
                      q_out, k_out, vt_out, iq_out, ik_out, iw_out):
    h = _rms_rows(x_ref[...], g1_ref[...]).astype(BF16)
    ca, s1a, s2a = ca_ref[...], s1a_ref[...], s2a_ref[...]
    ci, s1i, s2i = ci_ref[...], s1i_ref[...], s2i_ref[...]
    half_a = HEAD_DIM // ROPE_FRACTION_DIV // 2
    half_i = IDX_DIM // ROPE_FRACTION_DIV // 2

    def cols(j, n=1):
        return jnp.dot(h, w_ref[:, j * LANES:(j + n) * LANES], preferred_element_type=F32)

    qscale = HEAD_DIM ** -0.5 * LOG2_E
    for hd in range(N_HEADS):
        y = _rope(_rms_rows(cols(hd), qg_ref[...]), ca, s1a, s2a, half_a)
        q_out[:, hd * LANES:(hd + 1) * LANES] = (y * qscale).astype(BF16)
    base = N_HEADS
    for hd in range(N_KV_HEADS):
        y = _rope(_rms_rows(cols(base + hd), kg_ref[...]), ca, s1a, s2a, half_a)
        k_out[:, hd * LANES:(hd + 1) * LANES] = y.astype(BF16)
    base += N_KV_HEADS
    v_t = lax.dot_general(wvt_ref[...], h, NT_DIMS, preferred_element_type=F32)
    tk = vt_out.shape[-1]
    for j in range(vt_out.shape[0]):
        vt_out[j] = v_t[:, j * tk:(j + 1) * tk].astype(BF16)
    for hd in range(IDX_HEADS):
        y = _rope(cols(base + hd), ci, s1i, s2i, half_i)
        iq_out[:, hd * LANES:(hd + 1) * LANES] = y.astype(BF16)
    base += IDX_HEADS
    ik_out[...] = _rope(cols(base), ci, s1i, s2i, half_i).astype(BF16)
    iw_out[...] = cols(base + 1) * (IDX_HEADS ** -0.5 * IDX_DIM ** -0.5)


def _proj_a_kernel(x_ref, g1_ref, w_ref, a_out):
    h = _rms_rows(x_ref[...], g1_ref[...]).astype(BF16)
    cw = 2 * LANES
    for j in range(D_MODEL // cw):
        val = jnp.dot(h, w_ref[:, j * cw:(j + 1) * cw], preferred_element_type=F32)
        gate = jnp.dot(h, w_ref[:, D_MODEL + j * cw:D_MODEL + (j + 1) * cw], preferred_element_type=F32)
        a_out[:, j * cw:(j + 1) * cw] = (val * jax.nn.sigmoid(gate)).astype(BF16)


def _proj_c_kernel(x_ref, g1_ref, w_ref, cb_out, ccx_out):
    h = _rms_rows(x_ref[...], g1_ref[...]).astype(BF16)
    cw = 2 * LANES
    for j in range(D_MODEL // cw):
        cb = jnp.dot(h, w_ref[:, j * cw:(j + 1) * cw], preferred_element_type=F32)
        cc = jnp.dot(h, w_ref[:, D_MODEL + j * cw:D_MODEL + (j + 1) * cw], preferred_element_type=F32)
        cx = jnp.dot(h, w_ref[:, 2 * D_MODEL + j * cw:2 * D_MODEL + (j + 1) * cw], preferred_element_type=F32)
        cb_out[:, j * cw:(j + 1) * cw] = cb.astype(BF16)
        ccx_out[:, j * cw:(j + 1) * cw] = (cc * cx).astype(BF16)


def _proj_g_kernel(x_ref, g1_ref, w_ref, g_out):
    h = _rms_rows(x_ref[...], g1_ref[...]).astype(BF16)
    cw = 2 * LANES
    for j in range(3 * D_MODEL // cw):
        g = jnp.dot(h, w_ref[:, j * cw:(j + 1) * cw], preferred_element_type=F32)
        g_out[:, j * cw:(j + 1) * cw] = jax.nn.sigmoid(g).astype(BF16)


def _row_spec(tm, n):
    return pl.BlockSpec((tm, n), lambda i: (i, 0))


def _const_spec(shape):
    return pl.BlockSpec(shape, lambda i: (0,) * len(shape))


def _input_projections(x2, g1, w_att, w_vt, w_a, w_c, w_g, qg, kg, tabs_a, tabs_i, seq, tm, tk):
    m = x2.shape[0]
    grid = (m // tm,)
    tpb = seq // tm
    tab_spec = pl.BlockSpec((tm, LANES), lambda i: (i % tpb, 0))
    n_att = w_att.shape[1]
    q, k, vt, iq, ik, iw = pl.pallas_call(
        _proj_attn_kernel,
        grid=grid,
        in_specs=[_row_spec(tm, D_MODEL), _const_spec((1, D_MODEL)), _const_spec((D_MODEL, n_att)),
                  _const_spec((KV_COLS, D_MODEL)), _const_spec((1, LANES)), _const_spec((1, LANES))]
        + [tab_spec] * 6,
        out_specs=[_row_spec(tm, Q_COLS), _row_spec(tm, KV_COLS),
                   pl.BlockSpec((tm // tk, KV_COLS, tk), lambda i: (i, 0, 0)),
                   _row_spec(tm, IDX_HEADS * LANES), _row_spec(tm, LANES), _row_spec(tm, LANES)],
        out_shape=[jax.ShapeDtypeStruct((m, Q_COLS), BF16), jax.ShapeDtypeStruct((m, KV_COLS), BF16),
                   jax.ShapeDtypeStruct((m // tk, KV_COLS, tk), BF16),
                   jax.ShapeDtypeStruct((m, IDX_HEADS * LANES), BF16),
                   jax.ShapeDtypeStruct((m, LANES), BF16), jax.ShapeDtypeStruct((m, LANES), F32)],
        compiler_params=_cparams("parallel"),
        name="proj_attn",
    )(x2, g1, w_att, w_vt, qg, kg, *tabs_a, *tabs_i)
    a_act = pl.pallas_call(
        _proj_a_kernel, grid=grid,
        in_specs=[_row_spec(tm, D_MODEL), _const_spec((1, D_MODEL)), _const_spec((D_MODEL, 2 * D_MODEL))],
        out_specs=_row_spec(tm, D_MODEL),
        out_shape=jax.ShapeDtypeStruct((m, D_MODEL), BF16),
        compiler_params=_cparams("parallel"), name="proj_a",
    )(x2, g1, w_a)
    cb, ccx = pl.pallas_call(
        _proj_c_kernel, grid=grid,
        in_specs=[_row_spec(tm, D_MODEL), _const_spec((1, D_MODEL)), _const_spec((D_MODEL, 3 * D_MODEL))],
        out_specs=[_row_spec(tm, D_MODEL), _row_spec(tm, D_MODEL)],
        out_shape=[jax.ShapeDtypeStruct((m, D_MODEL), BF16)] * 2,
        compiler_params=_cparams("parallel"), name="proj_c",
    )(x2, g1, w_c)
    gates = pl.pallas_call(
        _proj_g_kernel, grid=grid,
        in_specs=[_row_spec(tm, D_MODEL), _const_spec((1, D_MODEL)), _const_spec((D_MODEL, 3 * D_MODEL))],
        out_specs=_row_spec(tm, 3 * D_MODEL),
        out_shape=jax.ShapeDtypeStruct((m, 3 * D_MODEL), BF16),
        compiler_params=_cparams("parallel"), name="proj_g",
    )(x2, g1, w_g)
    return q, k, vt, iq, ik, iw, a_act, cb, ccx, gates


def _dsa_kernel(iq_ref, ik_ref, iw_ref, q_ref, k_ref, vt_ref, o_ref, keys_ref, p_ref, acc_ref,
                bias_ref, lg_ref, pb_ref, *, tq, tk, seq, ksel):
    qi = pl.program_id(1)
    n_it = ((qi + 1) * tq + tk - 1) // tk
    iw_t = iw_ref[0].T
    t_idx = qi * tq + lax.broadcasted_iota(jnp.int32, (tk, tq), 1)
    row_iota = lax.broadcasted_iota(jnp.int32, (tk, tq), 0)

    def chunk_start(c):
        return pl.multiple_of(c * tk, tk)

    def score_chunk(c, carry):
        k0 = chunk_start(c)
        ikc = ik_ref[0, pl.ds(k0, tk), :]
        acc = jnp.zeros((tk, tq), F32)
        for hd in range(IDX_HEADS):
            iqh = iq_ref[0, :, hd * LANES:(hd + 1) * LANES]
            rel = lax.dot_general(ikc, iqh, NT_DIMS, preferred_element_type=F32)
            acc = acc + jnp.maximum(rel, 0.0) * iw_t[hd:hd + 1, :]
        bits = lax.bitcast_convert_type(acc, jnp.int32)
        key = jnp.where(bits < 0, -(bits & 0x7FFFFFFF), bits)
        key = jnp.where(k0 + row_iota <= t_idx, key, INT_MIN)
        keys_ref[pl.ds(k0, tk), :] = key
        return carry

    lax.fori_loop(0, n_it, score_chunk, 0)

    blk_iota = lax.broadcasted_iota(jnp.int32, (COUNT_ROWS, tq), 0)

    def count(pred):
        def body(c, acc8):
            k0 = chunk_start(c)
            for rb in range(tk // COUNT_ROWS):
                b0 = k0 + rb * COUNT_ROWS
                kk = keys_ref[pl.ds(pl.multiple_of(b0, COUNT_ROWS), COUNT_ROWS), :]
                m = jnp.where(pred(kk, b0 + blk_iota), 1.0, 0.0)
                acc8 = acc8 + m.reshape(COUNT_ROWS // SUBLANES, SUBLANES, tq).sum(axis=0)
            return acc8
        acc8 = lax.fori_loop(0, n_it, body, jnp.zeros((SUBLANES, tq), F32))
        return acc8.sum(axis=0, keepdims=True)

    kf = float(ksel)

    def bit_body(i, t_u):
        cand_u = t_u | jnp.left_shift(jnp.int32(1), 31 - i)
        cand_s = cand_u ^ INT_MIN
        cnt = count(lambda kk, k0: kk >= cand_s)
        return jnp.where(cnt >= kf, cand_u, t_u)

    thr = lax.fori_loop(0, 32, bit_body, jnp.zeros((1, tq), jnp.int32)) ^ INT_MIN

    c_gt = count(lambda kk, k0: kk > thr)
    c_eq = count(lambda kk, k0: kk == thr)
    need = kf - c_gt
    has_thr = thr != INT_MIN
    excess = jnp.logical_and(c_eq > need, has_thr)

    p_ref[...] = jnp.where(has_thr, seq, -1).astype(jnp.int32)

    @pl.when(jnp.max(jnp.where(excess, 1.0, 0.0)) > 0.0)
    def _():
        nbits = (seq - 1).bit_length()

        def tie_body(i, p):
            cand = p | jnp.left_shift(jnp.int32(1), nbits - 1 - i)
            cnt = count(lambda kk, rows: jnp.logical_and(kk == thr, rows < cand))
            return jnp.where(cnt < need, cand, p)

        p = lax.fori_loop(0, nbits, tie_body, jnp.zeros((1, tq), jnp.int32))
        p_ref[...] = jnp.where(has_thr, p, -1)

    p_last = p_ref[...]

    acc_ref[...] = jnp.zeros(acc_ref.shape, F32)

    def attend(c, stats):
        k0 = chunk_start(c)
        nblk = tk // COUNT_ROWS
        grp = COUNT_ROWS // SUBLANES

        def rows(rb):
            return slice(rb * COUNT_ROWS, (rb + 1) * COUNT_ROWS)

        for rb in range(nblk):
            b0 = k0 + rb * COUNT_ROWS
            kk = keys_ref[pl.ds(pl.multiple_of(b0, COUNT_ROWS), COUNT_ROWS), :]
            sel = jnp.logical_or(kk > thr, jnp.logical_and(kk == thr, b0 + blk_iota <= p_last))
            bias_ref[rows(rb), :] = jnp.where(sel, 0.0, MASK_NEG)
        new_stats = []

        def logits(hd):
            g = hd // HEADS_PER_KV
            kc = k_ref[0, pl.ds(k0, tk), g * LANES:(g + 1) * LANES]
            qh = q_ref[0, :, hd * LANES:(hd + 1) * LANES]
            lg = lax.dot_general(kc, qh, NT_DIMS, preferred_element_type=F32)
            cm = None
            for rb in range(nblk):
                blk = lg[rows(rb)] + bias_ref[rows(rb), :]
                lg_ref[hd % 2, rows(rb), :] = blk
                part = blk.reshape(grp, SUBLANES, tq).max(axis=0)
                cm = part if cm is None else jnp.maximum(cm, part)
            return cm

        def softmax_pv(hd, cm):
            g = hd // HEADS_PER_KV
            m_old, l_old = stats[hd]
            m_new = jnp.maximum(m_old, cm.max(axis=0, keepdims=True))
            m_b = jnp.broadcast_to(m_new, (SUBLANES, tq))
            ps = jnp.zeros((SUBLANES, tq), F32)
            for rb in range(nblk):
                p = jnp.exp2(lg_ref[hd % 2, rows(rb), :].reshape(grp, SUBLANES, tq) - m_b[None])
                ps = ps + p.sum(axis=0)
                pb_ref[hd % 2, rows(rb), :] = p.reshape(COUNT_ROWS, tq).astype(BF16)
            alpha = jnp.exp2(m_old - m_new)
            l_new = alpha * l_old + ps.sum(axis=0, keepdims=True)
            vtc = vt_ref[0, c, g * LANES:(g + 1) * LANES, :]
            pv = jnp.dot(vtc, pb_ref[hd % 2], preferred_element_type=F32)
            new_stats.append((m_new, l_new))
            return alpha, pv

        cms, pvs = {}, {}
        for s in range(N_HEADS + 2):
            if s < N_HEADS:
                cms[s] = logits(s)
            if 1 <= s <= N_HEADS:
                pvs[s - 1] = softmax_pv(s - 1, cms.pop(s - 1))
            if s >= 2:
                alpha, pv = pvs.pop(s - 2)
                acc_ref[s - 2] = alpha * acc_ref[s - 2] + pv
        return tuple(new_stats)

    init = tuple((jnp.full((1, tq), MASK_NEG, F32), jnp.zeros((1, tq), F32)) for _ in range(N_HEADS))
    stats = lax.fori_loop(0, n_it, attend, init)

    for hd in range(N_HEADS):
        out_t = acc_ref[hd] / stats[hd][1]
        o_ref[0, :, hd * LANES:(hd + 1) * LANES] = out_t.T.astype(BF16)


def _sparse_attention(iq, ik, iw, q, k, vt, tq, tk, ksel):
    b, seq, _ = iq.shape
    nq = seq // tq
    nc = seq // tk
    tile = lambda n: pl.BlockSpec((1, tq, n), lambda bi, qi: (bi, qi, 0))
    whole = lambda n: pl.BlockSpec((1, seq, n), lambda bi, qi: (bi, 0, 0))
    return pl.pallas_call(
        functools.partial(_dsa_kernel, tq=tq, tk=tk, seq=seq, ksel=ksel),
        grid=(b, nq),
        in_specs=[tile(IDX_HEADS * LANES), whole(LANES), tile(LANES), tile(Q_COLS), whole(KV_COLS),
                  pl.BlockSpec((1, nc, KV_COLS, tk), lambda bi, qi: (bi, 0, 0, 0))],
        out_specs=tile(Q_COLS),
        out_shape=jax.ShapeDtypeStruct((b, seq, Q_COLS), BF16),
        scratch_shapes=[pltpu.VMEM((seq, tq), jnp.int32), pltpu.VMEM((1, tq), jnp.int32),
                        pltpu.VMEM((N_HEADS, LANES, tq), F32), pltpu.VMEM((tk, tq), F32),
                        pltpu.VMEM((2, tk, tq), F32), pltpu.VMEM((2, tk, tq), BF16)],
        compiler_params=_cparams("parallel", "parallel"),
        name="dsa",
    )(iq, ik, iw, q, k, vt)


CONF_HALO = 32
SC_HALO = 16
CONV_ROWS = 64


def _mix_kernel(x_ref, a_ref, ah_ref, ccx_ref, ch_ref, cb_ref, g_ref, o_ref,
                cw_ref, cbias_ref, lng_ref, lnb_ref, wconf_ref, scw_ref, wsc_ref, wattn_ref, wo_ref,
                out_ref, xa_ref, xc_ref, ya_ref, yc_ref, *, tm):
    first = pl.program_id(1) == 0
    ah = ah_ref[0].astype(F32)
    xa_ref[0, 0:CONF_HALO, :] = jnp.where(first, 0.0, ah)
    xa_ref[0, CONF_HALO:, :] = a_ref[0].astype(F32)
    shifted_rows = CONF_HALO + tm - SUBLANES
    for r in range(1, SUBLANES):
        xa_ref[r, 0:shifted_rows, :] = xa_ref[0, r:r + shifted_rows, :]
    ch = ch_ref[0].astype(F32)
    xc_ref[0:SC_HALO, :] = jnp.where(first, 0.0, ch)
    xc_ref[SC_HALO:, :] = ccx_ref[0].astype(F32)

    def conv_lanes(lc, carry):
        ls = pl.ds(pl.multiple_of(lc * LANES, LANES), LANES)
        for rc in range(tm // CONV_ROWS):
            r0 = rc * CONV_ROWS
            acc = jnp.zeros((CONV_ROWS, LANES), F32)
            for j in range(CONF_KERNEL):
                off = CONF_HALO - (CONF_KERNEL - 1) + j
                base = r0 + off - off % SUBLANES
                acc = acc + xa_ref[off % SUBLANES, base:base + CONV_ROWS, ls] * cw_ref[j:j + 1, ls]
            ya_ref[r0:r0 + CONV_ROWS, ls] = acc + cbias_ref[:, ls]
            acc = jnp.zeros((CONV_ROWS, LANES), F32)
            for j in range(SC_KERNEL):
                off = SC_HALO - (SC_KERNEL - 1) + j
                acc = acc + xc_ref[r0 + off:r0 + off + CONV_ROWS, ls] * scw_ref[j:j + 1, ls]
            yc_ref[r0:r0 + CONV_ROWS, ls] = acc
        return carry

    lax.fori_loop(0, D_MODEL // LANES, conv_lanes, 0)

    ya = ya_ref[...]
    mu = jnp.mean(ya, axis=-1, keepdims=True)
    yc0 = ya - mu
    rs = lax.rsqrt(jnp.mean(yc0 * yc0, axis=-1, keepdims=True) + NORM_EPS)
    ln = yc0 * rs * lng_ref[...] + lnb_ref[...]
    act = (ln * jax.nn.sigmoid(ln)).astype(BF16)
    y_conf = jnp.dot(act, wconf_ref[...], preferred_element_type=F32)
    y_sc = jnp.dot((cb_ref[0].astype(F32) * yc_ref[...]).astype(BF16), wsc_ref[...], preferred_element_type=F32)
    y_attn = jnp.dot(o_ref[0], wattn_ref[...], preferred_element_type=F32)
    g = g_ref[0]
    merged = (g[:, 0:D_MODEL].astype(F32) * y_conf + g[:, D_MODEL:2 * D_MODEL].astype(F32) * y_attn
              + g[:, 2 * D_MODEL:3 * D_MODEL].astype(F32) * y_sc)
    out_ref[0] = x_ref[0] + jnp.dot(merged.astype(BF16), wo_ref[...], preferred_element_type=F32)


def _mix(x3, a_act, ccx, cb, gates, o, conf_w, conf_b, ln_g, ln_b, w_conf, sc_w, w_sc, w_attn, w_o, tm):
    b, seq, _ = x3.shape
    nt = seq // tm

    def tile(n):
        return pl.BlockSpec((1, tm, n), lambda bi, ti: (bi, ti, 0))

    def halo(rows):
        per = tm // rows
        return pl.BlockSpec((1, rows, D_MODEL), lambda bi, ti: (bi, jnp.maximum(ti * per - 1, 0), 0))

    def const(shape):
        return pl.BlockSpec(shape, lambda bi, ti: (0,) * len(shape))

    sq = (D_MODEL, D_MODEL)
    return pl.pallas_call(
        functools.partial(_mix_kernel, tm=tm),
        grid=(b, nt),
        in_specs=[tile(D_MODEL), tile(D_MODEL), halo(CONF_HALO), tile(D_MODEL), halo(SC_HALO), tile(D_MODEL),
                  tile(3 * D_MODEL), tile(Q_COLS),
                  const((CONF_KERNEL, D_MODEL)), const((1, D_MODEL)), const((1, D_MODEL)), const((1, D_MODEL)),
                  const(sq), const((SC_KERNEL, D_MODEL)), const(sq), const(sq), const(sq)],
        out_specs=tile(D_MODEL),
        out_shape=jax.ShapeDtypeStruct((b, seq, D_MODEL), F32),
        scratch_shapes=[pltpu.VMEM((SUBLANES, CONF_HALO + tm, D_MODEL), F32),
                        pltpu.VMEM((SC_HALO + tm, D_MODEL), F32),
                        pltpu.VMEM((tm, D_MODEL), F32), pltpu.VMEM((tm, D_MODEL), F32)],
        compiler_params=_cparams("parallel", "parallel"),
        name="mix",
    )(x3, a_act, a_act, ccx, ccx, cb, gates, o, conf_w, conf_b, ln_g, ln_b, w_conf, sc_w, w_sc, w_attn, w_o)


FFN_CHUNK = 256
FFN_CARRY = 8


def _ffn_kernel(x_ref, g2_ref, wg_ref, wv_ref, cwg_ref, cwv_ref, cbg_ref, cbv_ref, wd_ref, out_ref,
                h_ref, ug_ref, uv_ref, act_ref, carry_g, carry_v, *, tm, tiles_per_seq):
    first = (pl.program_id(0) % tiles_per_seq) == 0
    x = x_ref[...]
    h_ref[...] = _rms_rows(x, g2_ref[...]).astype(BF16)

    nf = FFN_DIM // FFN_CHUNK

    def chunk_cols(f):
        return pl.ds(pl.multiple_of(f * FFN_CHUNK, FFN_CHUNK), FFN_CHUNK)

    def up_stage(f, slot):
        cols = chunk_cols(f)
        h = h_ref[...]
        ug_ref[slot, FFN_CARRY:, :] = jnp.dot(h, wg_ref[:, cols], preferred_element_type=F32)
        uv_ref[slot, FFN_CARRY:, :] = jnp.dot(h, wv_ref[:, cols], preferred_element_type=F32)

    def conv_branch(f, slot, cols, u_ref, carry_ref, cw_ref, cb_ref):
        u_ref[slot, 0:FFN_CARRY, :] = jnp.where(first, 0.0, carry_ref[f])
        carry_ref[f] = u_ref[slot, tm:tm + FFN_CARRY, :]
        out = cb_ref[:, cols]
        for j in range(FFN_KERNEL):
            off = FFN_CARRY - (FFN_KERNEL - 1) + j
            out = out + u_ref[slot, pl.ds(off, tm), :] * cw_ref[j:j + 1, cols]
        return out

    def gate_stage(f, slot):
        cols = chunk_cols(f)
        ug = conv_branch(f, slot, cols, ug_ref, carry_g, cwg_ref, cbg_ref)
        uv = conv_branch(f, slot, cols, uv_ref, carry_v, cwv_ref, cbv_ref)
        act_ref[:, cols] = (ug * jax.nn.sigmoid(ug) * uv).astype(BF16)

    up_stage(0, 0)

    def chunk_pair(i, carry):
        f = 2 * i
        up_stage(f + 1, 1)
        gate_stage(f, 0)
        up_stage(f + 2, 0)
        gate_stage(f + 1, 1)
        return carry

    assert nf % 2 == 1
    lax.fori_loop(0, nf // 2, chunk_pair, 0)
    gate_stage(nf - 1, 0)
    out_ref[...] = x + jnp.dot(act_ref[...], wd_ref[...], preferred_element_type=F32)


def _ffn(x2, g2, w_up_g, w_up_v, cw_g, cw_v, cb_g, cb_v, w_down, seq, tm):
    m = x2.shape[0]
    nf = FFN_DIM // FFN_CHUNK

    def const(shape):
        return pl.BlockSpec(shape, lambda mi: (0, 0), pipeline_mode=pl.Buffered(1))

    return pl.pallas_call(
        functools.partial(_ffn_kernel, tm=tm, tiles_per_seq=seq // tm),
        grid=(m // tm,),
        in_specs=[pl.BlockSpec((tm, D_MODEL), lambda mi: (mi, 0)),
                  const((1, D_MODEL)),
                  const((D_MODEL, FFN_DIM)), const((D_MODEL, FFN_DIM)),
                  const((FFN_KERNEL, FFN_DIM)), const((FFN_KERNEL, FFN_DIM)),
                  const((1, FFN_DIM)), const((1, FFN_DIM)),
                  const((FFN_DIM, D_MODEL))],
        out_specs=pl.BlockSpec((tm, D_MODEL), lambda mi: (mi, 0)),
        out_shape=jax.ShapeDtypeStruct((m, D_MODEL), F32),
        scratch_shapes=[pltpu.VMEM((tm, D_MODEL), BF16),
                        pltpu.VMEM((2, FFN_CARRY + tm, FFN_CHUNK), F32),
                        pltpu.VMEM((2, FFN_CARRY + tm, FFN_CHUNK), F32),
                        pltpu.VMEM((tm, FFN_DIM), BF16),
                        pltpu.VMEM((nf, FFN_CARRY, FFN_CHUNK), F32), pltpu.VMEM((nf, FFN_CARRY, FFN_CHUNK), F32)],
        compiler_params=_cparams("arbitrary"),
        name="ffn",
    )(x2, g2, w_up_g, w_up_v, cw_g, cw_v, cb_g, cb_v, w_down)


def _rope_lane_tables(seq, rot_dim, period):
    half = rot_dim // 2
    pos = jnp.arange(seq, dtype=F32)
    inv_freq = jnp.power(ROPE_THETA, -jnp.arange(0, rot_dim, 2, dtype=F32) / rot_dim)
    ang = pos[:, None] * inv_freq[None, :]
    cos, sin = jnp.cos(ang), jnp.sin(ang)
    pad = period - rot_dim
    one = jnp.ones((seq, pad), F32)
    zero = jnp.zeros((seq, pad), F32)
    zh = jnp.zeros((seq, half), F32)
    c = jnp.concatenate([cos, cos, one], axis=1)
    s1 = jnp.concatenate([zh, sin, zero], axis=1)
    s2 = jnp.concatenate([-sin, zh, zero], axis=1)
    reps = LANES // period
    return tuple(jnp.tile(t, (1, reps)) for t in (c, s1, s2))


def _pad_cols(w, n):
    return jnp.pad(w, ((0, 0), (0, 0), (0, n - w.shape[-1])))


def _tile_rows(seq, target):
    t = min(target, seq)
    assert seq % t == 0
    return t


def kernel(x, norm1_g, w_in, q_norm_g, k_norm_g, w_attn_out, conf_conv_w, conf_conv_b, conf_ln_g, conf_ln_b,
           w_conf_out, sc_conv_w, w_sc_out, w_o, norm2_g, w_up, ffn_conv_w, ffn_conv_b, w_down):
    b, seq, d = x.shape
    depth = w_in.shape[0]
    assert d == D_MODEL and seq % LANES == 0
    tq = _tile_rows(seq, 256)
    tk = _tile_rows(seq, 512)
    tm_proj = _tile_rows(seq, 512)
    tm_mix = _tile_rows(seq, 256)
    tm_ffn = _tile_rows(seq, 512)
    ksel = min(TOPK_MAX, seq // 4)

    tabs_a = _rope_lane_tables(seq, HEAD_DIM // ROPE_FRACTION_DIV, LANES)
    tabs_i = _rope_lane_tables(seq, IDX_DIM // ROPE_FRACTION_DIV, LANES)

    offs = [0]
    for s in IN_SIZES:
        offs.append(offs[-1] + s)
    wq, wk, wv, wiq, wik, wiw, wa, wc, wg = (w_in[:, :, offs[i]:offs[i + 1]] for i in range(len(IN_SIZES)))
    wiq = jnp.pad(wiq.reshape(depth, d, IDX_HEADS, IDX_DIM), ((0, 0), (0, 0), (0, 0), (0, LANES - IDX_DIM)))
    wiq = wiq.reshape(depth, d, IDX_HEADS * LANES)
    w_att = jnp.concatenate([wq, wk, wiq, _pad_cols(wik, LANES), _pad_cols(wiw, LANES)], axis=-1).astype(BF16)
    layers = dict(
        g1=norm1_g[:, None, :], w_att=w_att, w_vt=jnp.swapaxes(wv, 1, 2).astype(BF16),
        w_a=wa.astype(BF16), w_c=wc.astype(BF16), w_g=wg.astype(BF16),
        qg=q_norm_g[:, None, :], kg=k_norm_g[:, None, :],
        conf_w=conf_conv_w, conf_b=conf_conv_b[:, None, :], ln_g=conf_ln_g[:, None, :], ln_b=conf_ln_b[:, None, :],
        w_conf=w_conf_out.astype(BF16), sc_w=sc_conv_w, w_sc=w_sc_out.astype(BF16),
        w_attn=w_attn_out.astype(BF16), w_o=w_o.astype(BF16),
        g2=norm2_g[:, None, :],
        w_up_g=w_up[:, :, :FFN_DIM].astype(BF16), w_up_v=w_up[:, :, FFN_DIM:].astype(BF16),
        cw_g=ffn_conv_w[:, :, :FFN_DIM], cw_v=ffn_conv_w[:, :, FFN_DIM:],
        cb_g=ffn_conv_b[:, None, :FFN_DIM], cb_v=ffn_conv_b[:, None, FFN_DIM:],
        w_down=w_down.astype(BF16),
    )

    def layer(xc, p):
        x2 = xc.reshape(b * seq, d)
        q, k, vt, iq, ik, iw, a_act, cb, ccx, gates = _input_projections(
            x2, p["g1"], p["w_att"], p["w_vt"], p["w_a"], p["w_c"], p["w_g"], p["qg"], p["kg"],
            tabs_a, tabs_i, seq, tm_proj, tk)
        r3 = lambda t: t.reshape(b, seq, t.shape[-1])
        o = _sparse_attention(r3(iq), r3(ik), r3(iw), r3(q), r3(k), vt.reshape(b, seq // tk, KV_COLS, tk),
                              tq, tk, ksel)
        x1 = _mix(xc, r3(a_act), r3(ccx), r3(cb), r3(gates), o, p["conf_w"], p["conf_b"], p["ln_g"], p["ln_b"],
                  p["w_conf"], p["sc_w"], p["w_sc"], p["w_attn"], p["w_o"], tm_mix)
        x_out = _ffn(x1.reshape(b * seq, d), p["g2"], p["w_up_g"], p["w_up_v"], p["cw_g"], p["cw_v"],
                     p["cb_g"], p["cb_v"], p["w_down"], seq, tm_ffn)
        return x_out.reshape(b, seq, d), None

    out, _ = lax.scan(layer, x, layers)
    return out
```

```python
import functools

import jax
import jax.numpy as jnp
from jax import lax
from jax.experimental import pallas as pl
from jax.experimental.pallas import tpu as pltpu

D_MODEL = 1024
N_HEADS = 8
HEAD_DIM = 128
N_KV_HEADS = 2
HEADS_PER_KV = N_HEADS // N_KV_HEADS
ROPE_FRACTION_DIV = 4
ROPE_THETA = 500000.0
IDX_HEADS = 8
IDX_DIM = 64
TOPK_MAX = 256
CONF_KERNEL = 31
SC_KERNEL = 3
FFN_DIM = 2816
FFN_KERNEL = 3
NORM_EPS = 1e-6

Q_COLS = N_HEADS * HEAD_DIM
KV_COLS = N_KV_HEADS * HEAD_DIM
IQ_COLS = IDX_HEADS * IDX_DIM
IN_SIZES = (Q_COLS, KV_COLS, KV_COLS, IQ_COLS, IDX_DIM, IDX_HEADS, 2 * D_MODEL, 3 * D_MODEL, 3 * D_MODEL)

LANES = 128
SUBLANES = 8
VMEM_LIMIT_BYTES = 56 * 1024 * 1024
MASK_NEG = -1e30
INT_MIN = -(2 ** 31)
LOG2_E = 1.4426950408889634

F32 = jnp.float32
BF16 = jnp.bfloat16
NT_DIMS = (((1,), (1,)), ((), ()))


def _cparams(*sem):
    return pltpu.CompilerParams(dimension_semantics=sem, vmem_limit_bytes=VMEM_LIMIT_BYTES)


def _rms_rows(x, g):
    ms = jnp.mean(x * x, axis=-1, keepdims=True)
    return x * lax.rsqrt(ms + NORM_EPS) * g


FOLD_WAYS = 4
COUNT_ROWS = 64
DIGIT_BITS = (14, 14)
PATTERN_BASE = 128
PACKED_ROWS = 2 * SUBLANES
VT_ROWS = HEAD_DIM + PACKED_ROWS


def _fold_rows(x, op):
    rows, n = x.shape
    part = op(x.reshape(FOLD_WAYS, rows // (8 * FOLD_WAYS), 8, n), axis=1)
    return op(part, axis=0)


def _rope(y, c, s1, s2, half):
    return y * c + pltpu.roll(y, half, 1) * s1 + pltpu.roll(y, LANES - half, 1) * s2


def _proj_attn_kernel(x_ref, g1_ref, w_ref, wvt_ref, qg_ref, kg_ref, ca_ref, s1a_ref, s2a_ref,
                      ci_ref, s1i_ref, s2i_ref,
                      q_out, k_out, vt_out, iq_out, ik_out, iw_out):
    h = _rms_rows(x_ref[...], g1_ref[...]).astype(BF16)
    ca, s1a, s2a = ca_ref[...], s1a_ref[...], s2a_ref[...]
    ci, s1i, s2i = ci_ref[...], s1i_ref[...], s2i_ref[...]
    half_a = HEAD_DIM // ROPE_FRACTION_DIV // 2
    half_i = IDX_DIM // ROPE_FRACTION_DIV // 2

    def cols(j, n=1):
        return jnp.dot(h, w_ref[:, j * LANES:(j + n) * LANES], preferred_element_type=F32)

    qscale = HEAD_DIM ** -0.5 * LOG2_E
    for hd in range(N_HEADS):
        y = _rope(_rms_rows(cols(hd), qg_ref[...]), ca, s1a, s2a, half_a)
        q_out[:, hd * LANES:(hd + 1) * LANES] = (y * qscale).astype(BF16)
    base = N_HEADS
    for hd in range(N_KV_HEADS):
        y = _rope(_rms_rows(cols(base + hd), kg_ref[...]), ca, s1a, s2a, half_a)
        k_out[:, hd * LANES:(hd + 1) * LANES] = y.astype(BF16)
    base += N_KV_HEADS
    v_t = lax.dot_general(wvt_ref[...], h, NT_DIMS, preferred_element_type=F32)
    tk = vt_out.shape[-1]
    ones_blk = jnp.ones((VT_ROWS - HEAD_DIM, tk), BF16)
    for j in range(vt_out.shape[0]):
        for g in range(N_KV_HEADS):
            vt_out[j, g * VT_ROWS:g * VT_ROWS + HEAD_DIM, :] = (
                v_t[g * HEAD_DIM:(g + 1) * HEAD_DIM, j * tk:(j + 1) * tk].astype(BF16))
            vt_out[j, g * VT_ROWS + HEAD_DIM:(g + 1) * VT_ROWS, :] = ones_blk
    for hd in range(IDX_HEADS):
        y = _rope(cols(base + hd), ci, s1i, s2i, half_i)
        iq_out[:, hd * LANES:(hd + 1) * LANES] = y.astype(BF16)
    base += IDX_HEADS
    ik_out[...] = _rope(cols(base), ci, s1i, s2i, half_i).astype(BF16)
    iw_out[...] = cols(base + 1) * (IDX_HEADS ** -0.5 * IDX_DIM ** -0.5)


def _proj_a_kernel(x_ref, g1_ref, w_ref, a_out):
    h = _rms_rows(x_ref[...], g1_ref[...]).astype(BF16)
    cw = 2 * LANES
    for j in range(D_MODEL // cw):
        val = jnp.dot(h, w_ref[:, j * cw:(j + 1) * cw], preferred_element_type=F32)
        gate = jnp.dot(h, w_ref[:, D_MODEL + j * cw:D_MODEL + (j + 1) * cw], preferred_element_type=F32)
        a_out[:, j * cw:(j + 1) * cw] = (val * jax.nn.sigmoid(gate)).astype(BF16)


def _proj_c_kernel(x_ref, g1_ref, w_ref, cb_out, ccx_out):
    h = _rms_rows(x_ref[...], g1_ref[...]).astype(BF16)
    cw = 2 * LANES
    for j in range(D_MODEL // cw):
        cb = jnp.dot(h, w_ref[:, j * cw:(j + 1) * cw], preferred_element_type=F32)
        cc = jnp.dot(h, w_ref[:, D_MODEL + j * cw:D_MODEL + (j + 1) * cw], preferred_element_type=F32)
        cx = jnp.dot(h, w_ref[:, 2 * D_MODEL + j * cw:2 * D_MODEL + (j + 1) * cw], preferred_element_type=F32)
        cb_out[:, j * cw:(j + 1) * cw] = cb.astype(BF16)
        ccx_out[:, j * cw:(j + 1) * cw] = (cc * cx).astype(BF16)


def _proj_g_kernel(x_ref, g1_ref, w_ref, g_out):
    h = _rms_rows(x_ref[...], g1_ref[...]).astype(BF16)
    cw = 2 * LANES
    for j in range(3 * D_MODEL // cw):
        g = jnp.dot(h, w_ref[:, j * cw:(j + 1) * cw], preferred_element_type=F32)
        g_out[:, j * cw:(j + 1) * cw] = jax.nn.sigmoid(g).astype(BF16)


def _row_spec(tm, n):
    return pl.BlockSpec((tm, n), lambda i: (i, 0))


def _const_spec(shape):
    return pl.BlockSpec(shape, lambda i: (0,) * len(shape))


def _input_projections(x2, g1, w_att, w_vt, w_a, w_c, w_g, qg, kg, tabs_a, tabs_i, seq, tm, tk):
    m = x2.shape[0]
    grid = (m // tm,)
    tpb = seq // tm
    tab_spec = pl.BlockSpec((tm, LANES), lambda i: (i % tpb, 0))
    n_att = w_att.shape[1]
    q, k, vt, iq, ik, iw = pl.pallas_call(
        _proj_attn_kernel,
        grid=grid,
        in_specs=[_row_spec(tm, D_MODEL), _const_spec((1, D_MODEL)), _const_spec((D_MODEL, n_att)),
                  _const_spec((KV_COLS, D_MODEL)), _const_spec((1, LANES)), _const_spec((1, LANES))]
        + [tab_spec] * 6,
        out_specs=[_row_spec(tm, Q_COLS), _row_spec(tm, KV_COLS),
                   pl.BlockSpec((tm // tk, N_KV_HEADS * VT_ROWS, tk), lambda i: (i, 0, 0)),
                   _row_spec(tm, IDX_HEADS * LANES), _row_spec(tm, LANES), _row_spec(tm, LANES)],
        out_shape=[jax.ShapeDtypeStruct((m, Q_COLS), BF16), jax.ShapeDtypeStruct((m, KV_COLS), BF16),
                   jax.ShapeDtypeStruct((m // tk, N_KV_HEADS * VT_ROWS, tk), BF16),
                   jax.ShapeDtypeStruct((m, IDX_HEADS * LANES), BF16),
                   jax.ShapeDtypeStruct((m, LANES), BF16), jax.ShapeDtypeStruct((m, LANES), F32)],
        compiler_params=_cparams("parallel"),
        name="proj_attn",
    )(x2, g1, w_att, w_vt, qg, kg, *tabs_a, *tabs_i)
    a_act = pl.pallas_call(
        _proj_a_kernel, grid=grid,
        in_specs=[_row_spec(tm, D_MODEL), _const_spec((1, D_MODEL)), _const_spec((D_MODEL, 2 * D_MODEL))],
        out_specs=_row_spec(tm, D_MODEL),
        out_shape=jax.ShapeDtypeStruct((m, D_MODEL), BF16),
        compiler_params=_cparams("parallel"), name="proj_a",
    )(x2, g1, w_a)
    cb, ccx = pl.pallas_call(
        _proj_c_kernel, grid=grid,
        in_specs=[_row_spec(tm, D_MODEL), _const_spec((1, D_MODEL)), _const_spec((D_MODEL, 3 * D_MODEL))],
        out_specs=[_row_spec(tm, D_MODEL), _row_spec(tm, D_MODEL)],
        out_shape=[jax.ShapeDtypeStruct((m, D_MODEL), BF16)] * 2,
        compiler_params=_cparams("parallel"), name="proj_c",
    )(x2, g1, w_c)
    gates = pl.pallas_call(
        _proj_g_kernel, grid=grid,
        in_specs=[_row_spec(tm, D_MODEL), _const_spec((1, D_MODEL)), _const_spec((D_MODEL, 3 * D_MODEL))],
        out_specs=_row_spec(tm, 3 * D_MODEL),
        out_shape=jax.ShapeDtypeStruct((m, 3 * D_MODEL), BF16),
        compiler_params=_cparams("parallel"), name="proj_g",
    )(x2, g1, w_g)
    return q, k, vt, iq, ik, iw, a_act, cb, ccx, gates


def _dsa_kernel(iq_ref, ik_ref, iw_ref, q_ref, k_ref, vt_ref, o_ref, keys_ref, kb_ref, acc_ref,
                bias_ref, lg_ref, pb_ref, *, tq, tk, seq, ksel):
    qi = pl.program_id(1)
    n_it = ((qi + 1) * tq + tk - 1) // tk
    iw_t = iw_ref[0].T
    t_idx = qi * tq + lax.broadcasted_iota(jnp.int32, (tk, tq), 1)
    row_iota = lax.broadcasted_iota(jnp.int32, (tk, tq), 0)

    def chunk_start(c):
        return pl.multiple_of(c * tk, tk)

    def score_chunk(c, carry):
        k0 = chunk_start(c)
        ikc = ik_ref[0, pl.ds(k0, tk), :]
        acc = jnp.zeros((tk, tq), F32)
        for hd in range(IDX_HEADS):
            iqh = iq_ref[0, :, hd * LANES:(hd + 1) * LANES]
            rel = lax.dot_general(ikc, iqh, NT_DIMS, preferred_element_type=F32)
            acc = acc + jnp.maximum(rel, 0.0) * iw_t[hd:hd + 1, :]
        bits = lax.bitcast_convert_type(acc, jnp.int32)
        key = jnp.where(bits < 0, -(bits & 0x7FFFFFFF), bits)
        key = jnp.where(k0 + row_iota <= t_idx, key, INT_MIN)
        keys_ref[pl.ds(k0, tk), :] = key
        return carry

    lax.fori_loop(0, n_it, score_chunk, 0)

    blk_iota = lax.broadcasted_iota(jnp.int32, (COUNT_ROWS, tq), 0)
    kf = float(ksel)
    packed_rows = PACKED_ROWS

    def to_pattern(v):
        return lax.bitcast_convert_type(v << 16, F32).astype(BF16)

    def write_patterns(ph, prefix):
        lo = 32 - sum(DIGIT_BITS[:ph + 1])
        mask = (1 << DIGIT_BITS[ph]) - 1

        def body(c, carry):
            k0 = chunk_start(c)
            for rb in range(tk // COUNT_ROWS):
                rows = pl.ds(pl.multiple_of(k0 + rb * COUNT_ROWS, COUNT_ROWS), COUNT_ROWS)
                kk = keys_ref[rows, :]
                if ph == 0:
                    pat = (((kk >> lo) & mask) ^ (1 << (DIGIT_BITS[0] - 1))) + PATTERN_BASE
                else:
                    pat = jnp.where((kk >> (lo + DIGIT_BITS[ph])) == prefix, ((kk >> lo) & mask) + PATTERN_BASE, 0)
                kb_ref[rows, :] = to_pattern(pat)
            return carry

        lax.fori_loop(0, n_it, body, 0)

    def count_ge(cand):
        cand_b = to_pattern(jnp.broadcast_to(cand + PATTERN_BASE, (packed_rows, tq)))
        groups = tk // packed_rows

        def body(c, acc):
            chunk = kb_ref[pl.ds(chunk_start(c), tk), :]
            parts = [None] * FOLD_WAYS
            for gi in range(groups):
                kb = chunk[gi * packed_rows:(gi + 1) * packed_rows]
                m = jnp.where(kb >= cand_b, jnp.ones_like(kb), jnp.zeros_like(kb))
                w = gi % FOLD_WAYS
                parts[w] = m if parts[w] is None else parts[w] + m
            total = parts[0]
            for w in range(1, FOLD_WAYS):
                total = total + parts[w]
            return acc + total.astype(F32)

        acc = lax.fori_loop(0, n_it, body, jnp.zeros((packed_rows, tq), F32))
        return acc.sum(axis=0, keepdims=True)

    def digit_search(ph, above):
        nbits = DIGIT_BITS[ph]

        def bit_body(i, t):
            cand = t | jnp.left_shift(jnp.int32(1), nbits - 1 - i)
            return jnp.where(above + count_ge(cand) >= kf, cand, t)

        t = lax.fori_loop(0, nbits, bit_body, jnp.zeros((1, tq), jnp.int32))
        return t, above + count_ge(t + 1)

    def count(pred):
        def body(c, acc8):
            k0 = chunk_start(c)
            for rb in range(tk // COUNT_ROWS):
                b0 = k0 + rb * COUNT_ROWS
                kk = keys_ref[pl.ds(pl.multiple_of(b0, COUNT_ROWS), COUNT_ROWS), :]
                m = jnp.where(pred(kk, b0 + blk_iota), 1.0, 0.0)
                acc8 = acc8 + m.reshape(COUNT_ROWS // SUBLANES, SUBLANES, tq).sum(axis=0)
            return acc8
        acc8 = lax.fori_loop(0, n_it, body, jnp.zeros((SUBLANES, tq), F32))
        return acc8.sum(axis=0, keepdims=True)

    write_patterns(0, None)
    t0, above = digit_search(0, jnp.zeros((1, tq), F32))
    prefix = t0 - (1 << (DIGIT_BITS[0] - 1))
    for ph in range(1, len(DIGIT_BITS)):
        write_patterns(ph, prefix)
        t, above = digit_search(ph, above)
        prefix = (prefix << DIGIT_BITS[ph]) | t
    low_bits = 32 - sum(DIGIT_BITS)

    def low_bit_body(i, t):
        cand = t | jnp.left_shift(jnp.int32(1), low_bits - 1 - i)
        cand_key = (prefix << low_bits) | cand
        return jnp.where(count(lambda kk, rows: kk >= cand_key) >= kf, cand, t)

    t_low = lax.fori_loop(0, low_bits, low_bit_body, jnp.zeros((1, tq), jnp.int32))
    thr = (prefix << low_bits) | t_low
    c_ge = count(lambda kk, rows: kk >= thr)
    has_thr = thr != INT_MIN
    excess = jnp.logical_and(c_ge > kf, has_thr)

    @pl.when(jnp.max(jnp.where(excess, 1.0, 0.0)) > 0.0)
    def _():
        nbits = (seq - 1).bit_length()
        need = kf - count(lambda kk, rows: kk > thr)

        def tie_body(i, p):
            cand = p | jnp.left_shift(jnp.int32(1), nbits - 1 - i)
            cnt = count(lambda kk, rows: jnp.logical_and(kk == thr, rows < cand))
            return jnp.where(cnt < need, cand, p)

        p_last = lax.fori_loop(0, nbits, tie_body, jnp.zeros((1, tq), jnp.int32))

        def demote(c, carry):
            k0 = chunk_start(c)
            for rb in range(tk // COUNT_ROWS):
                b0 = k0 + rb * COUNT_ROWS
                rows = pl.ds(pl.multiple_of(b0, COUNT_ROWS), COUNT_ROWS)
                kk = keys_ref[rows, :]
                drop = jnp.logical_and(jnp.logical_and(kk == thr, b0 + blk_iota > p_last), excess)
                keys_ref[rows, :] = jnp.where(drop, kk - 1, kk)
            return carry

        lax.fori_loop(0, n_it, demote, 0)

    thr_sel = jnp.where(has_thr, thr, INT_MIN + 1)

    acc_ref[...] = jnp.zeros(acc_ref.shape, F32)

    def attend(c, stats):
        k0 = chunk_start(c)
        nblk = tk // COUNT_ROWS
        grp = COUNT_ROWS // SUBLANES

        def rows(rb):
            return slice(rb * COUNT_ROWS, (rb + 1) * COUNT_ROWS)

        for rb in range(nblk):
            b0 = k0 + rb * COUNT_ROWS
            kk = keys_ref[pl.ds(pl.multiple_of(b0, COUNT_ROWS), COUNT_ROWS), :]
            bias_ref[rows(rb), :] = jnp.where(kk >= thr_sel, 0.0, MASK_NEG).astype(BF16)
        new_stats = []
        pgrp = COUNT_ROWS // PACKED_ROWS

        def logits(hd):
            g = hd // HEADS_PER_KV
            kc = k_ref[0, pl.ds(k0, tk), g * LANES:(g + 1) * LANES]
            qh = q_ref[0, :, hd * LANES:(hd + 1) * LANES]
            lg = lax.dot_general(kc, qh, NT_DIMS, preferred_element_type=F32)
            cm = None
            for rb in range(nblk):
                blk = lg[rows(rb)].astype(BF16) + bias_ref[rows(rb), :]
                lg_ref[hd % 2, rows(rb), :] = blk
                for gi in range(pgrp):
                    part = blk[gi * PACKED_ROWS:(gi + 1) * PACKED_ROWS]
                    cm = part if cm is None else jnp.maximum(cm, part)
            return cm

        def softmax_pv(hd, cm):
            g = hd // HEADS_PER_KV
            m_old, l_old = stats[hd]
            m_new = jnp.maximum(m_old, cm.astype(F32).max(axis=0, keepdims=True))
            m_b = jnp.broadcast_to(m_new, (PACKED_ROWS, tq)).astype(BF16)
            for rb in range(nblk):
                x = lg_ref[hd % 2, rows(rb), :].reshape(pgrp, PACKED_ROWS, tq)
                pb_ref[hd % 2, rows(rb), :] = jnp.exp2(x - m_b[None]).reshape(COUNT_ROWS, tq)
            alpha = jnp.exp2(m_old - m_new)
            vtc = vt_ref[0, c, g * VT_ROWS:(g + 1) * VT_ROWS, :]
            pv = jnp.dot(vtc, pb_ref[hd % 2], preferred_element_type=F32)
            l_new = alpha * l_old + pv[HEAD_DIM:HEAD_DIM + 1]
            new_stats.append((m_new, l_new))
            return alpha, pv[:HEAD_DIM]

        cms, pvs = {}, {}
        for s in range(N_HEADS + 2):
            if s < N_HEADS:
                cms[s] = logits(s)
            if 1 <= s <= N_HEADS:
                pvs[s - 1] = softmax_pv(s - 1, cms.pop(s - 1))
            if s >= 2:
                alpha, pv = pvs.pop(s - 2)
                acc_ref[s - 2] = alpha * acc_ref[s - 2] + pv
        return tuple(new_stats)

    init = tuple((jnp.full((1, tq), MASK_NEG, F32), jnp.zeros((1, tq), F32)) for _ in range(N_HEADS))
    stats = lax.fori_loop(0, n_it, attend, init)

    for hd in range(N_HEADS):
        out_t = acc_ref[hd] / stats[hd][1]
        o_ref[0, :, hd * LANES:(hd + 1) * LANES] = out_t.T.astype(BF16)


def _sparse_attention(iq, ik, iw, q, k, vt, tq, tk, ksel):
    b, seq, _ = iq.shape
    nq = seq // tq
    nc = seq // tk
    tile = lambda n: pl.BlockSpec((1, tq, n), lambda bi, qi: (bi, qi, 0))
    whole = lambda n: pl.BlockSpec((1, seq, n), lambda bi, qi: (bi, 0, 0))
    return pl.pallas_call(
        functools.partial(_dsa_kernel, tq=tq, tk=tk, seq=seq, ksel=ksel),
        grid=(b, nq),
        in_specs=[tile(IDX_HEADS * LANES), whole(LANES), tile(LANES), tile(Q_COLS), whole(KV_COLS),
                  pl.BlockSpec((1, nc, N_KV_HEADS * VT_ROWS, tk), lambda bi, qi: (bi, 0, 0, 0))],
        out_specs=tile(Q_COLS),
        out_shape=jax.ShapeDtypeStruct((b, seq, Q_COLS), BF16),
        scratch_shapes=[pltpu.VMEM((seq, tq), jnp.int32), pltpu.VMEM((seq, tq), BF16),
                        pltpu.VMEM((N_HEADS, LANES, tq), F32), pltpu.VMEM((tk, tq), BF16),
                        pltpu.VMEM((2, tk, tq), BF16), pltpu.VMEM((2, tk, tq), BF16)],
        compiler_params=_cparams("parallel", "parallel"),
        name="dsa",
    )(iq, ik, iw, q, k, vt)


CONF_HALO = 32
SC_HALO = 16
CONV_ROWS = 64


def _mix_kernel(x_ref, a_ref, ah_ref, ccx_ref, ch_ref, cb_ref, g_ref, o_ref,
                cw_ref, cbias_ref, lng_ref, lnb_ref, wconf_ref, scw_ref, wsc_ref, wattn_ref, wo_ref,
                out_ref, xa_ref, xc_ref, ya_ref, yc_ref, *, tm):
    first = pl.program_id(1) == 0
    ah = ah_ref[0].astype(F32)
    xa_ref[0, 0:CONF_HALO, :] = jnp.where(first, 0.0, ah)
    xa_ref[0, CONF_HALO:, :] = a_ref[0].astype(F32)
    shifted_rows = CONF_HALO + tm - SUBLANES
    for r in range(1, SUBLANES):
        xa_ref[r, 0:shifted_rows, :] = xa_ref[0, r:r + shifted_rows, :]
    ch = ch_ref[0].astype(F32)
    xc_ref[0:SC_HALO, :] = jnp.where(first, 0.0, ch)
    xc_ref[SC_HALO:, :] = ccx_ref[0].astype(F32)

    def conv_lanes(lc, carry):
        ls = pl.ds(pl.multiple_of(lc * LANES, LANES), LANES)
        for rc in range(tm // CONV_ROWS):
            r0 = rc * CONV_ROWS
            acc = jnp.zeros((CONV_ROWS, LANES), F32)
            for j in range(CONF_KERNEL):
                off = CONF_HALO - (CONF_KERNEL - 1) + j
                base = r0 + off - off % SUBLANES
                acc = acc + xa_ref[off % SUBLANES, base:base + CONV_ROWS, ls] * cw_ref[j:j + 1, ls]
            ya_ref[r0:r0 + CONV_ROWS, ls] = acc + cbias_ref[:, ls]
            acc = jnp.zeros((CONV_ROWS, LANES), F32)
            for j in range(SC_KERNEL):
                off = SC_HALO - (SC_KERNEL - 1) + j
                acc = acc + xc_ref[r0 + off:r0 + off + CONV_ROWS, ls] * scw_ref[j:j + 1, ls]
            yc_ref[r0:r0 + CONV_ROWS, ls] = acc
        return carry

    lax.fori_loop(0, D_MODEL // LANES, conv_lanes, 0)

    ya = ya_ref[...]
    mu = jnp.mean(ya, axis=-1, keepdims=True)
    yc0 = ya - mu
    rs = lax.rsqrt(jnp.mean(yc0 * yc0, axis=-1, keepdims=True) + NORM_EPS)
    ln = yc0 * rs * lng_ref[...] + lnb_ref[...]
    act = (ln * jax.nn.sigmoid(ln)).astype(BF16)
    y_conf = jnp.dot(act, wconf_ref[...], preferred_element_type=F32)
    y_sc = jnp.dot((cb_ref[0].astype(F32) * yc_ref[...]).astype(BF16), wsc_ref[...], preferred_element_type=F32)
    y_attn = jnp.dot(o_ref[0], wattn_ref[...], preferred_element_type=F32)
    g = g_ref[0]
    merged = (g[:, 0:D_MODEL].astype(F32) * y_conf + g[:, D_MODEL:2 * D_MODEL].astype(F32) * y_attn
              + g[:, 2 * D_MODEL:3 * D_MODEL].astype(F32) * y_sc)
    out_ref[0] = x_ref[0] + jnp.dot(merged.astype(BF16), wo_ref[...], preferred_element_type=F32)


def _mix(x3, a_act, ccx, cb, gates, o, conf_w, conf_b, ln_g, ln_b, w_conf, sc_w, w_sc, w_attn, w_o, tm):
    b, seq, _ = x3.shape
    nt = seq // tm

    def tile(n):
        return pl.BlockSpec((1, tm, n), lambda bi, ti: (bi, ti, 0))

    def halo(rows):
        per = tm // rows
        return pl.BlockSpec((1, rows, D_MODEL), lambda bi, ti: (bi, jnp.maximum(ti * per - 1, 0), 0))

    def const(shape):
        return pl.BlockSpec(shape, lambda bi, ti: (0,) * len(shape))

    sq = (D_MODEL, D_MODEL)
    return pl.pallas_call(
        functools.partial(_mix_kernel, tm=tm),
        grid=(b, nt),
        in_specs=[tile(D_MODEL), tile(D_MODEL), halo(CONF_HALO), tile(D_MODEL), halo(SC_HALO), tile(D_MODEL),
                  tile(3 * D_MODEL), tile(Q_COLS),
                  const((CONF_KERNEL, D_MODEL)), const((1, D_MODEL)), const((1, D_MODEL)), const((1, D_MODEL)),
                  const(sq), const((SC_KERNEL, D_MODEL)), const(sq), const(sq), const(sq)],
        out_specs=tile(D_MODEL),
        out_shape=jax.ShapeDtypeStruct((b, seq, D_MODEL), F32),
        scratch_shapes=[pltpu.VMEM((SUBLANES, CONF_HALO + tm, D_MODEL), F32),
                        pltpu.VMEM((SC_HALO + tm, D_MODEL), F32),
                        pltpu.VMEM((tm, D_MODEL), F32), pltpu.VMEM((tm, D_MODEL), F32)],
        compiler_params=_cparams("parallel", "parallel"),
        name="mix",
    )(x3, a_act, a_act, ccx, ccx, cb, gates, o, conf_w, conf_b, ln_g, ln_b, w_conf, sc_w, w_sc, w_attn, w_o)


FFN_CHUNK = 256
FFN_CARRY = 8


def _ffn_kernel(x_ref, g2_ref, wg_ref, wv_ref, cwg_ref, cwv_ref, cbg_ref, cbv_ref, wd_ref, out_ref,
                h_ref, ug_ref, uv_ref, act_ref, carry_g, carry_v, *, tm, tiles_per_seq):
    first = (pl.program_id(0) % tiles_per_seq) == 0
    x = x_ref[...]
    h_ref[...] = _rms_rows(x, g2_ref[...]).astype(BF16)

    nf = FFN_DIM // FFN_CHUNK

    def chunk_cols(f):
        return pl.ds(pl.multiple_of(f * FFN_CHUNK, FFN_CHUNK), FFN_CHUNK)

    def up_stage(f, slot):
        cols = chunk_cols(f)
        h = h_ref[...]
        ug_ref[slot, FFN_CARRY:, :] = jnp.dot(h, wg_ref[:, cols], preferred_element_type=F32)
        uv_ref[slot, FFN_CARRY:, :] = jnp.dot(h, wv_ref[:, cols], preferred_element_type=F32)

    def conv_branch(f, slot, cols, u_ref, carry_ref, cw_ref, cb_ref):
        u_ref[slot, 0:FFN_CARRY, :] = jnp.where(first, 0.0, carry_ref[f])
        carry_ref[f] = u_ref[slot, tm:tm + FFN_CARRY, :]
        out = cb_ref[:, cols]
        for j in range(FFN_KERNEL):
            off = FFN_CARRY - (FFN_KERNEL - 1) + j
            out = out + u_ref[slot, pl.ds(off, tm), :] * cw_ref[j:j + 1, cols]
        return out

    def gate_stage(f, slot):
        cols = chunk_cols(f)
        ug = conv_branch(f, slot, cols, ug_ref, carry_g, cwg_ref, cbg_ref)
        uv = conv_branch(f, slot, cols, uv_ref, carry_v, cwv_ref, cbv_ref)
        act_ref[:, cols] = (ug * jax.nn.sigmoid(ug) * uv).astype(BF16)

    up_stage(0, 0)

    def chunk_pair(i, carry):
        f = 2 * i
        up_stage(f + 1, 1)
        gate_stage(f, 0)
        up_stage(f + 2, 0)
        gate_stage(f + 1, 1)
        return carry

    assert nf % 2 == 1
    lax.fori_loop(0, nf // 2, chunk_pair, 0)
    gate_stage(nf - 1, 0)
    out_ref[...] = x + jnp.dot(act_ref[...], wd_ref[...], preferred_element_type=F32)


def _ffn(x2, g2, w_up_g, w_up_v, cw_g, cw_v, cb_g, cb_v, w_down, seq, tm):
    m = x2.shape[0]
    nf = FFN_DIM // FFN_CHUNK

    def const(shape):
        return pl.BlockSpec(shape, lambda mi: (0, 0), pipeline_mode=pl.Buffered(1))

    return pl.pallas_call(
        functools.partial(_ffn_kernel, tm=tm, tiles_per_seq=seq // tm),
        grid=(m // tm,),
        in_specs=[pl.BlockSpec((tm, D_MODEL), lambda mi: (mi, 0)),
                  const((1, D_MODEL)),
                  const((D_MODEL, FFN_DIM)), const((D_MODEL, FFN_DIM)),
                  const((FFN_KERNEL, FFN_DIM)), const((FFN_KERNEL, FFN_DIM)),
                  const((1, FFN_DIM)), const((1, FFN_DIM)),
                  const((FFN_DIM, D_MODEL))],
        out_specs=pl.BlockSpec((tm, D_MODEL), lambda mi: (mi, 0)),
        out_shape=jax.ShapeDtypeStruct((m, D_MODEL), F32),
        scratch_shapes=[pltpu.VMEM((tm, D_MODEL), BF16),
                        pltpu.VMEM((2, FFN_CARRY + tm, FFN_CHUNK), F32),
                        pltpu.VMEM((2, FFN_CARRY + tm, FFN_CHUNK), F32),
                        pltpu.VMEM((tm, FFN_DIM), BF16),
                        pltpu.VMEM((nf, FFN_CARRY, FFN_CHUNK), F32), pltpu.VMEM((nf, FFN_CARRY, FFN_CHUNK), F32)],
        compiler_params=_cparams("arbitrary"),
        name="ffn",
    )(x2, g2, w_up_g, w_up_v, cw_g, cw_v, cb_g, cb_v, w_down)


def _rope_lane_tables(seq, rot_dim, period):
    half = rot_dim // 2
    pos = jnp.arange(seq, dtype=F32)
    inv_freq = jnp.power(ROPE_THETA, -jnp.arange(0, rot_dim, 2, dtype=F32) / rot_dim)
    ang = pos[:, None] * inv_freq[None, :]
    cos, sin = jnp.cos(ang), jnp.sin(ang)
    pad = period - rot_dim
    one = jnp.ones((seq, pad), F32)
    zero = jnp.zeros((seq, pad), F32)
    zh = jnp.zeros((seq, half), F32)
    c = jnp.concatenate([cos, cos, one], axis=1)
    s1 = jnp.concatenate([zh, sin, zero], axis=1)
    s2 = jnp.concatenate([-sin, zh, zero], axis=1)
    reps = LANES // period
    return tuple(jnp.tile(t, (1, reps)) for t in (c, s1, s2))


def _pad_cols(w, n):
    return jnp.pad(w, ((0, 0), (0, 0), (0, n - w.shape[-1])))


def _tile_rows(seq, target):
    t = min(target, seq)
    assert seq % t == 0
    return t


def kernel(x, norm1_g, w_in, q_norm_g, k_norm_g, w_attn_out, conf_conv_w, conf_conv_b, conf_ln_g, conf_ln_b,
           w_conf_out, sc_conv_w, w_sc_out, w_o, norm2_g, w_up, ffn_conv_w, ffn_conv_b, w_down):
    b, seq, d = x.shape
    depth = w_in.shape[0]
    assert d == D_MODEL and seq % LANES == 0
    tq = _tile_rows(seq, 256)
    tk = _tile_rows(seq, 512)
    tm_proj = _tile_rows(seq, 512)
    tm_mix = _tile_rows(seq, 256)
    tm_ffn = _tile_rows(seq, 512)
    ksel = min(TOPK_MAX, seq // 4)

    tabs_a = _rope_lane_tables(seq, HEAD_DIM // ROPE_FRACTION_DIV, LANES)
    tabs_i = _rope_lane_tables(seq, IDX_DIM // ROPE_FRACTION_DIV, LANES)

    offs = [0]
    for s in IN_SIZES:
        offs.append(offs[-1] + s)
    wq, wk, wv, wiq, wik, wiw, wa, wc, wg = (w_in[:, :, offs[i]:offs[i + 1]] for i in range(len(IN_SIZES)))
    wiq = jnp.pad(wiq.reshape(depth, d, IDX_HEADS, IDX_DIM), ((0, 0), (0, 0), (0, 0), (0, LANES - IDX_DIM)))
    wiq = wiq.reshape(depth, d, IDX_HEADS * LANES)
    w_att = jnp.concatenate([wq, wk, wiq, _pad_cols(wik, LANES), _pad_cols(wiw, LANES)], axis=-1).astype(BF16)
    layers = dict(
        g1=norm1_g[:, None, :], w_att=w_att, w_vt=jnp.swapaxes(wv, 1, 2).astype(BF16),
        w_a=wa.astype(BF16), w_c=wc.astype(BF16), w_g=wg.astype(BF16),
        qg=q_norm_g[:, None, :], kg=k_norm_g[:, None, :],
        conf_w=conf_conv_w, conf_b=conf_conv_b[:, None, :], ln_g=conf_ln_g[:, None, :], ln_b=conf_ln_b[:, None, :],
        w_conf=w_conf_out.astype(BF16), sc_w=sc_conv_w, w_sc=w_sc_out.astype(BF16),
        w_attn=w_attn_out.astype(BF16), w_o=w_o.astype(BF16),
        g2=norm2_g[:, None, :],
        w_up_g=w_up[:, :, :FFN_DIM].astype(BF16), w_up_v=w_up[:, :, FFN_DIM:].astype(BF16),
        cw_g=ffn_conv_w[:, :, :FFN_DIM], cw_v=ffn_conv_w[:, :, FFN_DIM:],
        cb_g=ffn_conv_b[:, None, :FFN_DIM], cb_v=ffn_conv_b[:, None, FFN_DIM:],
        w_down=w_down.astype(BF16),
    )

    def layer(xc, p):
        x2 = xc.reshape(b * seq, d)
        q, k, vt, iq, ik, iw, a_act, cb, ccx, gates = _input_projections(
            x2, p["g1"], p["w_att"], p["w_vt"], p["w_a"], p["w_c"], p["w_g"], p["qg"], p["kg"],
            tabs_a, tabs_i, seq, tm_proj, tk)
        r3 = lambda t: t.reshape(b, seq, t.shape[-1])
        o = _sparse_attention(r3(iq), r3(ik), r3(iw), r3(q), r3(k), vt.reshape(b, seq // tk, N_KV_HEADS * VT_ROWS, tk),
                              tq, tk, ksel)
        x1 = _mix(xc, r3(a_act), r3(ccx), r3(cb), r3(gates), o, p["conf_w"], p["conf_b"], p["ln_g"], p["ln_b"],
                  p["w_conf"], p["sc_w"], p["w_sc"], p["w_attn"], p["w_o"], tm_mix)
        x_out = _ffn(x1.reshape(b * seq, d), p["g2"], p["w_up_g"], p["w_up_v"], p["cw_g"], p["cw_v"],
                     p["cb_g"], p["cb_v"], p["w_down"], seq, tm_ffn)
        return x_out.reshape(b, seq, d), None

    out, _ = lax.scan(layer, x, layers)
    return out
```

```python
import functools

import jax
import jax.numpy as jnp
from jax import lax
from jax.experimental import pallas as pl
from jax.experimental.pallas import tpu as pltpu

D_MODEL = 1024
N_HEADS = 8
HEAD_DIM = 128
N_KV_HEADS = 2
HEADS_PER_KV = N_HEADS // N_KV_HEADS
ROPE_FRACTION_DIV = 4
ROPE_THETA = 500000.0
IDX_HEADS = 8
IDX_DIM = 64
TOPK_MAX = 256
CONF_KERNEL = 31
SC_KERNEL = 3
FFN_DIM = 2816
FFN_KERNEL = 3
NORM_EPS = 1e-6

Q_COLS = N_HEADS * HEAD_DIM
KV_COLS = N_KV_HEADS * HEAD_DIM
IQ_COLS = IDX_HEADS * IDX_DIM
IN_SIZES = (Q_COLS, KV_COLS, KV_COLS, IQ_COLS, IDX_DIM, IDX_HEADS, 2 * D_MODEL, 3 * D_MODEL, 3 * D_MODEL)

LANES = 128
SUBLANES = 8
VMEM_LIMIT_BYTES = 56 * 1024 * 1024
MASK_NEG = -1e30
INT_MIN = -(2 ** 31)
LOG2_E = 1.4426950408889634

F32 = jnp.float32
BF16 = jnp.bfloat16
NT_DIMS = (((1,), (1,)), ((), ()))


def _cparams(*sem):
    return pltpu.CompilerParams(dimension_semantics=sem, vmem_limit_bytes=VMEM_LIMIT_BYTES)


def _rms_rows(x, g):
    ms = jnp.mean(x * x, axis=-1, keepdims=True)
    return x * lax.rsqrt(ms + NORM_EPS) * g


FOLD_WAYS = 4
COUNT_ROWS = 64
DIGIT_BITS = (14, 14)
PATTERN_BASE = 128
PACKED_ROWS = 2 * SUBLANES
VT_ROWS = HEAD_DIM + PACKED_ROWS


def _fold_rows(x, op):
    rows, n = x.shape
    part = op(x.reshape(FOLD_WAYS, rows // (8 * FOLD_WAYS), 8, n), axis=1)
    return op(part, axis=0)


def _rope(y, c, s):
    return y * c + pltpu.roll(y, LANES // 2, 1) * s


def _proj_attn_kernel(x_ref, g1_ref, w_ref, wvt_ref, qg_ref, kg_ref, ca_ref, sa_ref, ci_ref, si_ref,
                      q_out, k_out, vt_out, iq_out, ik_out, iw_out):
    h = _rms_rows(x_ref[...], g1_ref[...]).astype(BF16)
    ca, sa = ca_ref[...], sa_ref[...]
    ci, si = ci_ref[...], si_ref[...]

    qscale = HEAD_DIM ** -0.5 * LOG2_E

    def q_head(hd, y):
        y = _rope(_rms_rows(y, qg_ref[...]), ca, sa)
        q_out[:, hd * LANES:(hd + 1) * LANES] = (y * qscale).astype(BF16)

    def k_head(hd, y):
        y = _rope(_rms_rows(y, kg_ref[...]), ca, sa)
        k_out[:, hd * LANES:(hd + 1) * LANES] = y.astype(BF16)

    def iq_head(hd, y):
        iq_out[:, hd * LANES:(hd + 1) * LANES] = _rope(y, ci, si).astype(BF16)

    def ik_tile(_, y):
        ik_out[...] = _rope(y, ci, si).astype(BF16)

    def iw_tile(_, y):
        iw_out[...] = y * (IDX_HEADS ** -0.5 * IDX_DIM ** -0.5)

    tiles = ([(q_head, i) for i in range(N_HEADS)] + [(k_head, i) for i in range(N_KV_HEADS)]
             + [(iq_head, i) for i in range(IDX_HEADS)] + [(ik_tile, 0), (iw_tile, 0)])

    def pair(p):
        return jnp.dot(h, w_ref[:, 2 * p * LANES:(2 * p + 2) * LANES], preferred_element_type=F32)

    nxt = pair(0)
    for p in range(len(tiles) // 2):
        cur = nxt
        if 2 * p + 2 < len(tiles):
            nxt = pair(p + 1)
        for half in range(2):
            fn, arg = tiles[2 * p + half]
            fn(arg, cur[:, half * LANES:(half + 1) * LANES])
    v_t = lax.dot_general(wvt_ref[...], h, NT_DIMS, preferred_element_type=F32)
    tk = vt_out.shape[-1]
    ones_blk = jnp.ones((VT_ROWS - HEAD_DIM, tk), BF16)
    for j in range(vt_out.shape[0]):
        for g in range(N_KV_HEADS):
            vt_out[j, g * VT_ROWS:g * VT_ROWS + HEAD_DIM, :] = (
                v_t[g * HEAD_DIM:(g + 1) * HEAD_DIM, j * tk:(j + 1) * tk].astype(BF16))
            vt_out[j, g * VT_ROWS + HEAD_DIM:(g + 1) * VT_ROWS, :] = ones_blk


def _proj_a_kernel(x_ref, g1_ref, w_ref, a_out):
    h = _rms_rows(x_ref[...], g1_ref[...]).astype(BF16)
    cw = 2 * LANES
    for j in range(D_MODEL // cw):
        val = jnp.dot(h, w_ref[:, j * cw:(j + 1) * cw], preferred_element_type=F32)
        gate = jnp.dot(h, w_ref[:, D_MODEL + j * cw:D_MODEL + (j + 1) * cw], preferred_element_type=F32)
        a_out[:, j * cw:(j + 1) * cw] = (val * jax.nn.sigmoid(gate)).astype(BF16)


def _proj_c_kernel(x_ref, g1_ref, w_ref, cb_out, ccx_out):
    h = _rms_rows(x_ref[...], g1_ref[...]).astype(BF16)
    cw = 2 * LANES
    for j in range(D_MODEL // cw):
        cb = jnp.dot(h, w_ref[:, j * cw:(j + 1) * cw], preferred_element_type=F32)
        cc = jnp.dot(h, w_ref[:, D_MODEL + j * cw:D_MODEL + (j + 1) * cw], preferred_element_type=F32)
        cx = jnp.dot(h, w_ref[:, 2 * D_MODEL + j * cw:2 * D_MODEL + (j + 1) * cw], preferred_element_type=F32)
        cb_out[:, j * cw:(j + 1) * cw] = cb.astype(BF16)
        ccx_out[:, j * cw:(j + 1) * cw] = (cc * cx).astype(BF16)


def _proj_g_kernel(x_ref, g1_ref, w_ref, g_out):
    h = _rms_rows(x_ref[...], g1_ref[...]).astype(BF16)
    cw = 2 * LANES
    for j in range(3 * D_MODEL // cw):
        g = jnp.dot(h, w_ref[:, j * cw:(j + 1) * cw], preferred_element_type=F32)
        g_out[:, j * cw:(j + 1) * cw] = jax.nn.sigmoid(g).astype(BF16)


def _row_spec(tm, n):
    return pl.BlockSpec((tm, n), lambda i: (i, 0))


def _const_spec(shape):
    return pl.BlockSpec(shape, lambda i: (0,) * len(shape))


def _input_projections(x2, g1, w_att, w_vt, w_a, w_c, w_g, qg, kg, tabs_a, tabs_i, seq, tm, tk):
    m = x2.shape[0]
    grid = (m // tm,)
    tpb = seq // tm
    tab_spec = pl.BlockSpec((tm, LANES), lambda i: (i % tpb, 0))
    n_att = w_att.shape[1]
    q, k, vt, iq, ik, iw = pl.pallas_call(
        _proj_attn_kernel,
        grid=grid,
        in_specs=[_row_spec(tm, D_MODEL), _const_spec((1, D_MODEL)), _const_spec((D_MODEL, n_att)),
                  _const_spec((KV_COLS, D_MODEL)), _const_spec((1, LANES)), _const_spec((1, LANES))]
        + [tab_spec] * 4,
        out_specs=[_row_spec(tm, Q_COLS), _row_spec(tm, KV_COLS),
                   pl.BlockSpec((tm // tk, N_KV_HEADS * VT_ROWS, tk), lambda i: (i, 0, 0)),
                   _row_spec(tm, IDX_HEADS * LANES), _row_spec(tm, LANES), _row_spec(tm, LANES)],
        out_shape=[jax.ShapeDtypeStruct((m, Q_COLS), BF16), jax.ShapeDtypeStruct((m, KV_COLS), BF16),
                   jax.ShapeDtypeStruct((m // tk, N_KV_HEADS * VT_ROWS, tk), BF16),
                   jax.ShapeDtypeStruct((m, IDX_HEADS * LANES), BF16),
                   jax.ShapeDtypeStruct((m, LANES), BF16), jax.ShapeDtypeStruct((m, LANES), F32)],
        compiler_params=_cparams("parallel"),
        name="proj_attn",
    )(x2, g1, w_att, w_vt, qg, kg, *tabs_a, *tabs_i)
    a_act = pl.pallas_call(
        _proj_a_kernel, grid=grid,
        in_specs=[_row_spec(tm, D_MODEL), _const_spec((1, D_MODEL)), _const_spec((D_MODEL, 2 * D_MODEL))],
        out_specs=_row_spec(tm, D_MODEL),
        out_shape=jax.ShapeDtypeStruct((m, D_MODEL), BF16),
        compiler_params=_cparams("parallel"), name="proj_a",
    )(x2, g1, w_a)
    cb, ccx = pl.pallas_call(
        _proj_c_kernel, grid=grid,
        in_specs=[_row_spec(tm, D_MODEL), _const_spec((1, D_MODEL)), _const_spec((D_MODEL, 3 * D_MODEL))],
        out_specs=[_row_spec(tm, D_MODEL), _row_spec(tm, D_MODEL)],
        out_shape=[jax.ShapeDtypeStruct((m, D_MODEL), BF16)] * 2,
        compiler_params=_cparams("parallel"), name="proj_c",
    )(x2, g1, w_c)
    gates = pl.pallas_call(
        _proj_g_kernel, grid=grid,
        in_specs=[_row_spec(tm, D_MODEL), _const_spec((1, D_MODEL)), _const_spec((D_MODEL, 3 * D_MODEL))],
        out_specs=_row_spec(tm, 3 * D_MODEL),
        out_shape=jax.ShapeDtypeStruct((m, 3 * D_MODEL), BF16),
        compiler_params=_cparams("parallel"), name="proj_g",
    )(x2, g1, w_g)
    return q, k, vt, iq, ik, iw, a_act, cb, ccx, gates


def _dsa_kernel(iq_ref, ik_ref, iw_ref, q_ref, k_ref, vt_ref, o_ref, keys_ref, kb_ref, acc_ref,
                bias_ref, lg_ref, pb_ref, *, tq, tk, seq, ksel):
    qi = pl.program_id(1)
    n_it = ((qi + 1) * tq + tk - 1) // tk
    iw_t = iw_ref[0].T
    t_idx = qi * tq + lax.broadcasted_iota(jnp.int32, (tk, tq), 1)
    row_iota = lax.broadcasted_iota(jnp.int32, (tk, tq), 0)

    def chunk_start(c):
        return pl.multiple_of(c * tk, tk)

    def score_chunk(c, carry):
        k0 = chunk_start(c)
        ikc = ik_ref[0, pl.ds(k0, tk), :]
        acc = jnp.zeros((tk, tq), F32)
        for hd in range(IDX_HEADS):
            iqh = iq_ref[0, :, hd * LANES:(hd + 1) * LANES]
            rel = lax.dot_general(ikc, iqh, NT_DIMS, preferred_element_type=F32)
            acc = acc + jnp.maximum(rel, 0.0) * iw_t[hd:hd + 1, :]
        bits = lax.bitcast_convert_type(acc, jnp.int32)
        key = jnp.where(bits < 0, -(bits & 0x7FFFFFFF), bits)
        key = jnp.where(k0 + row_iota <= t_idx, key, INT_MIN)
        keys_ref[pl.ds(k0, tk), :] = key
        return carry

    lax.fori_loop(0, n_it, score_chunk, 0)

    blk_iota = lax.broadcasted_iota(jnp.int32, (COUNT_ROWS, tq), 0)
    kf = float(ksel)
    packed_rows = PACKED_ROWS

    def to_pattern(v):
        return lax.bitcast_convert_type(v << 16, F32).astype(BF16)

    def write_patterns(ph, prefix):
        lo = 32 - sum(DIGIT_BITS[:ph + 1])
        mask = (1 << DIGIT_BITS[ph]) - 1

        def body(c, carry):
            k0 = chunk_start(c)
            for rb in range(tk // COUNT_ROWS):
                rows = pl.ds(pl.multiple_of(k0 + rb * COUNT_ROWS, COUNT_ROWS), COUNT_ROWS)
                kk = keys_ref[rows, :]
                if ph == 0:
                    pat = (((kk >> lo) & mask) ^ (1 << (DIGIT_BITS[0] - 1))) + PATTERN_BASE
                else:
                    pat = jnp.where((kk >> (lo + DIGIT_BITS[ph])) == prefix, ((kk >> lo) & mask) + PATTERN_BASE, 0)
                kb_ref[rows, :] = to_pattern(pat)
            return carry

        lax.fori_loop(0, n_it, body, 0)

    def count_ge(cand):
        cand_b = to_pattern(jnp.broadcast_to(cand + PATTERN_BASE, (packed_rows, tq)))
        groups = tk // packed_rows

        def body(c, acc):
            chunk = kb_ref[pl.ds(chunk_start(c), tk), :]
            parts = [None] * FOLD_WAYS
            for gi in range(groups):
                kb = chunk[gi * packed_rows:(gi + 1) * packed_rows]
                m = jnp.where(kb >= cand_b, jnp.ones_like(kb), jnp.zeros_like(kb))
                w = gi % FOLD_WAYS
                parts[w] = m if parts[w] is None else parts[w] + m
            total = parts[0]
            for w in range(1, FOLD_WAYS):
                total = total + parts[w]
            return acc + total.astype(F32)

        acc = lax.fori_loop(0, n_it, body, jnp.zeros((packed_rows, tq), F32))
        return acc.sum(axis=0, keepdims=True)

    def digit_search(ph, above):
        nbits = DIGIT_BITS[ph]

        def bit_body(i, t):
            cand = t | jnp.left_shift(jnp.int32(1), nbits - 1 - i)
            return jnp.where(above + count_ge(cand) >= kf, cand, t)

        t = lax.fori_loop(0, nbits, bit_body, jnp.zeros((1, tq), jnp.int32))
        return t, above + count_ge(t + 1)

    def count(pred):
        def body(c, acc8):
            k0 = chunk_start(c)
            for rb in range(tk // COUNT_ROWS):
                b0 = k0 + rb * COUNT_ROWS
                kk = keys_ref[pl.ds(pl.multiple_of(b0, COUNT_ROWS), COUNT_ROWS), :]
                m = jnp.where(pred(kk, b0 + blk_iota), 1.0, 0.0)
                acc8 = acc8 + m.reshape(COUNT_ROWS // SUBLANES, SUBLANES, tq).sum(axis=0)
            return acc8
        acc8 = lax.fori_loop(0, n_it, body, jnp.zeros((SUBLANES, tq), F32))
        return acc8.sum(axis=0, keepdims=True)

    write_patterns(0, None)
    t0, above = digit_search(0, jnp.zeros((1, tq), F32))
    prefix = t0 - (1 << (DIGIT_BITS[0] - 1))
    for ph in range(1, len(DIGIT_BITS)):
        write_patterns(ph, prefix)
        t, above = digit_search(ph, above)
        prefix = (prefix << DIGIT_BITS[ph]) | t
    low_bits = 32 - sum(DIGIT_BITS)

    def low_bit_body(i, t):
        cand = t | jnp.left_shift(jnp.int32(1), low_bits - 1 - i)
        cand_key = (prefix << low_bits) | cand
        return jnp.where(count(lambda kk, rows: kk >= cand_key) >= kf, cand, t)

    t_low = lax.fori_loop(0, low_bits, low_bit_body, jnp.zeros((1, tq), jnp.int32))
    thr = (prefix << low_bits) | t_low
    c_ge = count(lambda kk, rows: kk >= thr)
    has_thr = thr != INT_MIN
    excess = jnp.logical_and(c_ge > kf, has_thr)

    @pl.when(jnp.max(jnp.where(excess, 1.0, 0.0)) > 0.0)
    def _():
        nbits = (seq - 1).bit_length()
        need = kf - count(lambda kk, rows: kk > thr)

        def tie_body(i, p):
            cand = p | jnp.left_shift(jnp.int32(1), nbits - 1 - i)
            cnt = count(lambda kk, rows: jnp.logical_and(kk == thr, rows < cand))
            return jnp.where(cnt < need, cand, p)

        p_last = lax.fori_loop(0, nbits, tie_body, jnp.zeros((1, tq), jnp.int32))

        def demote(c, carry):
            k0 = chunk_start(c)
            for rb in range(tk // COUNT_ROWS):
                b0 = k0 + rb * COUNT_ROWS
                rows = pl.ds(pl.multiple_of(b0, COUNT_ROWS), COUNT_ROWS)
                kk = keys_ref[rows, :]
                drop = jnp.logical_and(jnp.logical_and(kk == thr, b0 + blk_iota > p_last), excess)
                keys_ref[rows, :] = jnp.where(drop, kk - 1, kk)
            return carry

        lax.fori_loop(0, n_it, demote, 0)

    thr_sel = jnp.where(has_thr, thr, INT_MIN + 1)

    acc_ref[...] = jnp.zeros(acc_ref.shape, F32)

    def attend(c, stats):
        k0 = chunk_start(c)
        nblk = tk // COUNT_ROWS
        grp = COUNT_ROWS // SUBLANES

        def rows(rb):
            return slice(rb * COUNT_ROWS, (rb + 1) * COUNT_ROWS)

        for rb in range(nblk):
            b0 = k0 + rb * COUNT_ROWS
            kk = keys_ref[pl.ds(pl.multiple_of(b0, COUNT_ROWS), COUNT_ROWS), :]
            bias_ref[rows(rb), :] = jnp.where(kk >= thr_sel, 0.0, MASK_NEG).astype(BF16)
        new_stats = []
        pgrp = COUNT_ROWS // PACKED_ROWS

        def logits(hd):
            g = hd // HEADS_PER_KV
            kc = k_ref[0, pl.ds(k0, tk), g * LANES:(g + 1) * LANES]
            qh = q_ref[0, :, hd * LANES:(hd + 1) * LANES]
            lg = lax.dot_general(kc, qh, NT_DIMS, preferred_element_type=F32)
            cm = None
            for rb in range(nblk):
                blk = lg[rows(rb)].astype(BF16) + bias_ref[rows(rb), :]
                lg_ref[hd % 2, rows(rb), :] = blk
                for gi in range(pgrp):
                    part = blk[gi * PACKED_ROWS:(gi + 1) * PACKED_ROWS]
                    cm = part if cm is None else jnp.maximum(cm, part)
            return cm

        def softmax_pv(hd, cm):
            g = hd // HEADS_PER_KV
            m_old, l_old = stats[hd]
            m_new = jnp.maximum(m_old, cm.astype(F32).max(axis=0, keepdims=True))
            m_b = jnp.broadcast_to(m_new, (PACKED_ROWS, tq)).astype(BF16)
            for rb in range(nblk):
                x = lg_ref[hd % 2, rows(rb), :].reshape(pgrp, PACKED_ROWS, tq)
                pb_ref[hd % 2, rows(rb), :] = jnp.exp2(x - m_b[None]).reshape(COUNT_ROWS, tq)
            alpha = jnp.exp2(m_old - m_new)
            vtc = vt_ref[0, c, g * VT_ROWS:(g + 1) * VT_ROWS, :]
            pv = jnp.dot(vtc, pb_ref[hd % 2], preferred_element_type=F32)
            l_new = alpha * l_old + pv[HEAD_DIM:HEAD_DIM + 1]
            new_stats.append((m_new, l_new))
            return alpha, pv[:HEAD_DIM]

        cms, pvs = {}, {}
        for s in range(N_HEADS + 2):
            if s < N_HEADS:
                cms[s] = logits(s)
            if 1 <= s <= N_HEADS:
                pvs[s - 1] = softmax_pv(s - 1, cms.pop(s - 1))
            if s >= 2:
                alpha, pv = pvs.pop(s - 2)
                acc_ref[s - 2] = alpha * acc_ref[s - 2] + pv
        return tuple(new_stats)

    init = tuple((jnp.full((1, tq), MASK_NEG, F32), jnp.zeros((1, tq), F32)) for _ in range(N_HEADS))
    stats = lax.fori_loop(0, n_it, attend, init)

    for hd in range(N_HEADS):
        out_t = acc_ref[hd] / stats[hd][1]
        o_ref[0, :, hd * LANES:(hd + 1) * LANES] = out_t.T.astype(BF16)


def _sparse_attention(iq, ik, iw, q, k, vt, tq, tk, ksel):
    b, seq, _ = iq.shape
    nq = seq // tq
    nc = seq // tk
    tile = lambda n: pl.BlockSpec((1, tq, n), lambda bi, qi: (bi, qi, 0))
    whole = lambda n: pl.BlockSpec((1, seq, n), lambda bi, qi: (bi, 0, 0))
    return pl.pallas_call(
        functools.partial(_dsa_kernel, tq=tq, tk=tk, seq=seq, ksel=ksel),
        grid=(b, nq),
        in_specs=[tile(IDX_HEADS * LANES), whole(LANES), tile(LANES), tile(Q_COLS), whole(KV_COLS),
                  pl.BlockSpec((1, nc, N_KV_HEADS * VT_ROWS, tk), lambda bi, qi: (bi, 0, 0, 0))],
        out_specs=tile(Q_COLS),
        out_shape=jax.ShapeDtypeStruct((b, seq, Q_COLS), BF16),
        scratch_shapes=[pltpu.VMEM((seq, tq), jnp.int32), pltpu.VMEM((seq, tq), BF16),
                        pltpu.VMEM((N_HEADS, LANES, tq), F32), pltpu.VMEM((tk, tq), BF16),
                        pltpu.VMEM((2, tk, tq), BF16), pltpu.VMEM((2, tk, tq), BF16)],
        compiler_params=_cparams("parallel", "parallel"),
        name="dsa",
    )(iq, ik, iw, q, k, vt)


CONF_HALO = 32
SC_HALO = 16
CONV_ROWS = 64


def _mix_kernel(x_ref, a_ref, ah_ref, ccx_ref, ch_ref, cb_ref, g_ref, o_ref,
                cw_ref, cbias_ref, lng_ref, lnb_ref, wconf_ref, scw_ref, wsc_ref, wattn_ref, wo_ref,
                out_ref, xa_ref, xc_ref, ya_ref, yc_ref, *, tm):
    first = pl.program_id(1) == 0
    ah = ah_ref[0].astype(F32)
    xa_ref[0, 0:CONF_HALO, :] = jnp.where(first, 0.0, ah)
    xa_ref[0, CONF_HALO:, :] = a_ref[0].astype(F32)
    shifted_rows = CONF_HALO + tm - SUBLANES
    for r in range(1, SUBLANES):
        xa_ref[r, 0:shifted_rows, :] = xa_ref[0, r:r + shifted_rows, :]
    ch = ch_ref[0].astype(F32)
    xc_ref[0:SC_HALO, :] = jnp.where(first, 0.0, ch)
    xc_ref[SC_HALO:, :] = ccx_ref[0].astype(F32)

    def conv_lanes(lc, carry):
        ls = pl.ds(pl.multiple_of(lc * LANES, LANES), LANES)
        for rc in range(tm // CONV_ROWS):
            r0 = rc * CONV_ROWS
            acc = jnp.zeros((CONV_ROWS, LANES), F32)
            for j in range(CONF_KERNEL):
                off = CONF_HALO - (CONF_KERNEL - 1) + j
                base = r0 + off - off % SUBLANES
                acc = acc + xa_ref[off % SUBLANES, base:base + CONV_ROWS, ls] * cw_ref[j:j + 1, ls]
            ya_ref[r0:r0 + CONV_ROWS, ls] = acc + cbias_ref[:, ls]
            acc = jnp.zeros((CONV_ROWS, LANES), F32)
            for j in range(SC_KERNEL):
                off = SC_HALO - (SC_KERNEL - 1) + j
                acc = acc + xc_ref[r0 + off:r0 + off + CONV_ROWS, ls] * scw_ref[j:j + 1, ls]
            yc_ref[r0:r0 + CONV_ROWS, ls] = acc
        return carry

    lax.fori_loop(0, D_MODEL // LANES, conv_lanes, 0)

    ya = ya_ref[...]
    mu = jnp.mean(ya, axis=-1, keepdims=True)
    yc0 = ya - mu
    rs = lax.rsqrt(jnp.mean(yc0 * yc0, axis=-1, keepdims=True) + NORM_EPS)
    ln = yc0 * rs * lng_ref[...] + lnb_ref[...]
    act = (ln * jax.nn.sigmoid(ln)).astype(BF16)
    y_conf = jnp.dot(act, wconf_ref[...], preferred_element_type=F32)
    y_sc = jnp.dot((cb_ref[0].astype(F32) * yc_ref[...]).astype(BF16), wsc_ref[...], preferred_element_type=F32)
    y_attn = jnp.dot(o_ref[0], wattn_ref[...], preferred_element_type=F32)
    g = g_ref[0]
    merged = (g[:, 0:D_MODEL].astype(F32) * y_conf + g[:, D_MODEL:2 * D_MODEL].astype(F32) * y_attn
              + g[:, 2 * D_MODEL:3 * D_MODEL].astype(F32) * y_sc)
    out_ref[0] = x_ref[0] + jnp.dot(merged.astype(BF16), wo_ref[...], preferred_element_type=F32)


def _mix(x3, a_act, ccx, cb, gates, o, conf_w, conf_b, ln_g, ln_b, w_conf, sc_w, w_sc, w_attn, w_o, tm):
    b, seq, _ = x3.shape
    nt = seq // tm

    def tile(n):
        return pl.BlockSpec((1, tm, n), lambda bi, ti: (bi, ti, 0))

    def halo(rows):
        per = tm // rows
        return pl.BlockSpec((1, rows, D_MODEL), lambda bi, ti: (bi, jnp.maximum(ti * per - 1, 0), 0))

    def const(shape):
        return pl.BlockSpec(shape, lambda bi, ti: (0,) * len(shape))

    sq = (D_MODEL, D_MODEL)
    return pl.pallas_call(
        functools.partial(_mix_kernel, tm=tm),
        grid=(b, nt),
        in_specs=[tile(D_MODEL), tile(D_MODEL), halo(CONF_HALO), tile(D_MODEL), halo(SC_HALO), tile(D_MODEL),
                  tile(3 * D_MODEL), tile(Q_COLS),
                  const((CONF_KERNEL, D_MODEL)), const((1, D_MODEL)), const((1, D_MODEL)), const((1, D_MODEL)),
                  const(sq), const((SC_KERNEL, D_MODEL)), const(sq), const(sq), const(sq)],
        out_specs=tile(D_MODEL),
        out_shape=jax.ShapeDtypeStruct((b, seq, D_MODEL), F32),
        scratch_shapes=[pltpu.VMEM((SUBLANES, CONF_HALO + tm, D_MODEL), F32),
                        pltpu.VMEM((SC_HALO + tm, D_MODEL), F32),
                        pltpu.VMEM((tm, D_MODEL), F32), pltpu.VMEM((tm, D_MODEL), F32)],
        compiler_params=_cparams("parallel", "parallel"),
        name="mix",
    )(x3, a_act, a_act, ccx, ccx, cb, gates, o, conf_w, conf_b, ln_g, ln_b, w_conf, sc_w, w_sc, w_attn, w_o)


FFN_CHUNK = 256
FFN_CARRY = 8


def _ffn_kernel(x_ref, g2_ref, wg_ref, wv_ref, cwg_ref, cwv_ref, cbg_ref, cbv_ref, wd_ref, out_ref,
                h_ref, ug_ref, uv_ref, act_ref, carry_g, carry_v, *, tm, tiles_per_seq):
    first = (pl.program_id(0) % tiles_per_seq) == 0
    x = x_ref[...]
    h_ref[...] = _rms_rows(x, g2_ref[...]).astype(BF16)

    nf = FFN_DIM // FFN_CHUNK

    def chunk_cols(f):
        return pl.ds(pl.multiple_of(f * FFN_CHUNK, FFN_CHUNK), FFN_CHUNK)

    def up_stage(f, slot):
        cols = chunk_cols(f)
        h = h_ref[...]
        ug_ref[slot, FFN_CARRY:, :] = jnp.dot(h, wg_ref[:, cols], preferred_element_type=F32)
        uv_ref[slot, FFN_CARRY:, :] = jnp.dot(h, wv_ref[:, cols], preferred_element_type=F32)

    def conv_branch(f, slot, cols, u_ref, carry_ref, cw_ref, cb_ref):
        u_ref[slot, 0:FFN_CARRY, :] = jnp.where(first, 0.0, carry_ref[f])
        carry_ref[f] = u_ref[slot, tm:tm + FFN_CARRY, :]
        out = cb_ref[:, cols]
        for j in range(FFN_KERNEL):
            off = FFN_CARRY - (FFN_KERNEL - 1) + j
            out = out + u_ref[slot, pl.ds(off, tm), :] * cw_ref[j:j + 1, cols]
        return out

    def gate_stage(f, slot):
        cols = chunk_cols(f)
        ug = conv_branch(f, slot, cols, ug_ref, carry_g, cwg_ref, cbg_ref)
        uv = conv_branch(f, slot, cols, uv_ref, carry_v, cwv_ref, cbv_ref)
        act_ref[:, cols] = (ug * jax.nn.sigmoid(ug) * uv).astype(BF16)

    up_stage(0, 0)

    def chunk_pair(i, carry):
        f = 2 * i
        up_stage(f + 1, 1)
        gate_stage(f, 0)
        up_stage(f + 2, 0)
        gate_stage(f + 1, 1)
        return carry

    assert nf % 2 == 1
    lax.fori_loop(0, nf // 2, chunk_pair, 0)
    gate_stage(nf - 1, 0)
    out_ref[...] = x + jnp.dot(act_ref[...], wd_ref[...], preferred_element_type=F32)


def _ffn(x2, g2, w_up_g, w_up_v, cw_g, cw_v, cb_g, cb_v, w_down, seq, tm):
    m = x2.shape[0]
    nf = FFN_DIM // FFN_CHUNK

    def const(shape):
        return pl.BlockSpec(shape, lambda mi: (0, 0), pipeline_mode=pl.Buffered(1))

    return pl.pallas_call(
        functools.partial(_ffn_kernel, tm=tm, tiles_per_seq=seq // tm),
        grid=(m // tm,),
        in_specs=[pl.BlockSpec((tm, D_MODEL), lambda mi: (mi, 0)),
                  const((1, D_MODEL)),
                  const((D_MODEL, FFN_DIM)), const((D_MODEL, FFN_DIM)),
                  const((FFN_KERNEL, FFN_DIM)), const((FFN_KERNEL, FFN_DIM)),
                  const((1, FFN_DIM)), const((1, FFN_DIM)),
                  const((FFN_DIM, D_MODEL))],
        out_specs=pl.BlockSpec((tm, D_MODEL), lambda mi: (mi, 0)),
        out_shape=jax.ShapeDtypeStruct((m, D_MODEL), F32),
        scratch_shapes=[pltpu.VMEM((tm, D_MODEL), BF16),
                        pltpu.VMEM((2, FFN_CARRY + tm, FFN_CHUNK), F32),
                        pltpu.VMEM((2, FFN_CARRY + tm, FFN_CHUNK), F32),
                        pltpu.VMEM((tm, FFN_DIM), BF16),
                        pltpu.VMEM((nf, FFN_CARRY, FFN_CHUNK), F32), pltpu.VMEM((nf, FFN_CARRY, FFN_CHUNK), F32)],
        compiler_params=_cparams("arbitrary"),
        name="ffn",
    )(x2, g2, w_up_g, w_up_v, cw_g, cw_v, cb_g, cb_v, w_down)


def _rope_layout(dim, rot_dim):
    half = rot_dim // 2
    rest = list(range(rot_dim, dim))
    lo_fill = LANES // 2 - half
    lanes = list(range(half)) + rest[:lo_fill] + [-1] * max(0, lo_fill - len(rest))
    lanes += list(range(half, rot_dim)) + rest[lo_fill:]
    return lanes + [-1] * (LANES - len(lanes))


def _to_layout(w, layout):
    idx = jnp.asarray([max(i, 0) for i in layout])
    keep = jnp.asarray([1.0 if i >= 0 else 0.0 for i in layout], w.dtype)
    return jnp.take(w, idx, axis=-1) * keep


def _rope_lane_tables(seq, rot_dim):
    half = rot_dim // 2
    pos = jnp.arange(seq, dtype=F32)
    inv_freq = jnp.power(ROPE_THETA, -jnp.arange(0, rot_dim, 2, dtype=F32) / rot_dim)
    ang = pos[:, None] * inv_freq[None, :]
    cos, sin = jnp.cos(ang), jnp.sin(ang)
    one = jnp.ones((seq, LANES // 2 - half), F32)
    zero = jnp.zeros((seq, LANES // 2 - half), F32)
    c = jnp.concatenate([cos, one, cos, one], axis=1)
    s = jnp.concatenate([-sin, zero, sin, zero], axis=1)
    return c, s


def _pad_cols(w, n):
    return jnp.pad(w, ((0, 0), (0, 0), (0, n - w.shape[-1])))


def _tile_rows(seq, target):
    t = min(target, seq)
    assert seq % t == 0
    return t


def kernel(x, norm1_g, w_in, q_norm_g, k_norm_g, w_attn_out, conf_conv_w, conf_conv_b, conf_ln_g, conf_ln_b,
           w_conf_out, sc_conv_w, w_sc_out, w_o, norm2_g, w_up, ffn_conv_w, ffn_conv_b, w_down):
    b, seq, d = x.shape
    depth = w_in.shape[0]
    assert d == D_MODEL and seq % LANES == 0
    tq = _tile_rows(seq, 256)
    tk = _tile_rows(seq, 512)
    tm_proj = _tile_rows(seq, 512)
    tm_mix = _tile_rows(seq, 256)
    tm_ffn = _tile_rows(seq, 512)
    ksel = min(TOPK_MAX, seq // 4)

    rot_a, rot_i = HEAD_DIM // ROPE_FRACTION_DIV, IDX_DIM // ROPE_FRACTION_DIV
    tabs_a = _rope_lane_tables(seq, rot_a)
    tabs_i = _rope_lane_tables(seq, rot_i)
    lay_a, lay_i = _rope_layout(HEAD_DIM, rot_a), _rope_layout(IDX_DIM, rot_i)

    def heads_to_layout(w, n_heads, dim, layout):
        return _to_layout(w.reshape(w.shape[:-1] + (n_heads, dim)), layout).reshape(w.shape[:-1] + (n_heads * LANES,))

    offs = [0]
    for s in IN_SIZES:
        offs.append(offs[-1] + s)
    wq, wk, wv, wiq, wik, wiw, wa, wc, wg = (w_in[:, :, offs[i]:offs[i + 1]] for i in range(len(IN_SIZES)))
    w_att = jnp.concatenate([heads_to_layout(wq, N_HEADS, HEAD_DIM, lay_a),
                             heads_to_layout(wk, N_KV_HEADS, HEAD_DIM, lay_a),
                             heads_to_layout(wiq, IDX_HEADS, IDX_DIM, lay_i),
                             _to_layout(wik, lay_i), _pad_cols(wiw, LANES)], axis=-1).astype(BF16)
    layers = dict(
        g1=norm1_g[:, None, :], w_att=w_att, w_vt=jnp.swapaxes(wv, 1, 2).astype(BF16),
        w_a=wa.astype(BF16), w_c=wc.astype(BF16), w_g=wg.astype(BF16),
        qg=_to_layout(q_norm_g, lay_a)[:, None, :], kg=_to_layout(k_norm_g, lay_a)[:, None, :],
        conf_w=conf_conv_w, conf_b=conf_conv_b[:, None, :], ln_g=conf_ln_g[:, None, :], ln_b=conf_ln_b[:, None, :],
        w_conf=w_conf_out.astype(BF16), sc_w=sc_conv_w, w_sc=w_sc_out.astype(BF16),
        w_attn=w_attn_out.astype(BF16), w_o=w_o.astype(BF16),
        g2=norm2_g[:, None, :],
        w_up_g=w_up[:, :, :FFN_DIM].astype(BF16), w_up_v=w_up[:, :, FFN_DIM:].astype(BF16),
        cw_g=ffn_conv_w[:, :, :FFN_DIM], cw_v=ffn_conv_w[:, :, FFN_DIM:],
        cb_g=ffn_conv_b[:, None, :FFN_DIM], cb_v=ffn_conv_b[:, None, FFN_DIM:],
        w_down=w_down.astype(BF16),
    )

    def layer(xc, p):
        x2 = xc.reshape(b * seq, d)
        q, k, vt, iq, ik, iw, a_act, cb, ccx, gates = _input_projections(
            x2, p["g1"], p["w_att"], p["w_vt"], p["w_a"], p["w_c"], p["w_g"], p["qg"], p["kg"],
            tabs_a, tabs_i, seq, tm_proj, tk)
        r3 = lambda t: t.reshape(b, seq, t.shape[-1])
        o = _sparse_attention(r3(iq), r3(ik), r3(iw), r3(q), r3(k), vt.reshape(b, seq // tk, N_KV_HEADS * VT_ROWS, tk),
                              tq, tk, ksel)
        x1 = _mix(xc, r3(a_act), r3(ccx), r3(cb), r3(gates), o, p["conf_w"], p["conf_b"], p["ln_g"], p["ln_b"],
                  p["w_conf"], p["sc_w"], p["w_sc"], p["w_attn"], p["w_o"], tm_mix)
        x_out = _ffn(x1.reshape(b * seq, d), p["g2"], p["w_up_g"], p["w_up_v"], p["cw_g"], p["cw_v"],
                     p["cb_g"], p["cb_v"], p["w_down"], seq, tm_ffn)
        return x_out.reshape(b, seq, d), None

    out, _ = lax.scan(layer, x, layers)
    return out
```

```python
import functools

import jax
import jax.numpy as jnp
from jax import lax
from jax.experimental import pallas as pl
from jax.experimental.pallas import tpu as pltpu

D_MODEL = 1024
N_HEADS = 8
HEAD_DIM = 128
N_KV_HEADS = 2
HEADS_PER_KV = N_HEADS // N_KV_HEADS
ROPE_FRACTION_DIV = 4
ROPE_THETA = 500000.0
IDX_HEADS = 8
IDX_DIM = 64
TOPK_MAX = 256
CONF_KERNEL = 31
SC_KERNEL = 3
FFN_DIM = 2816
FFN_KERNEL = 3
NORM_EPS = 1e-6

Q_COLS = N_HEADS * HEAD_DIM
KV_COLS = N_KV_HEADS * HEAD_DIM
IQ_COLS = IDX_HEADS * IDX_DIM
IN_SIZES = (Q_COLS, KV_COLS, KV_COLS, IQ_COLS, IDX_DIM, IDX_HEADS, 2 * D_MODEL, 3 * D_MODEL, 3 * D_MODEL)

LANES = 128
SUBLANES = 8
VMEM_LIMIT_BYTES = 56 * 1024 * 1024
MASK_NEG = -1e30
INT_MIN = -(2 ** 31)
LOG2_E = 1.4426950408889634

F32 = jnp.float32
BF16 = jnp.bfloat16
NT_DIMS = (((1,), (1,)), ((), ()))


def _cparams(*sem):
    return pltpu.CompilerParams(dimension_semantics=sem, vmem_limit_bytes=VMEM_LIMIT_BYTES)


def _rms_rows(x, g):
    ms = jnp.mean(x * x, axis=-1, keepdims=True)
    return x * lax.rsqrt(ms + NORM_EPS) * g


FOLD_WAYS = 4
COUNT_ROWS = 64
DIGIT_BITS = (14, 14)
PATTERN_BASE = 128
PACKED_ROWS = 2 * SUBLANES
VT_ROWS = HEAD_DIM + PACKED_ROWS


def _fold_rows(x, op):
    rows, n = x.shape
    part = op(x.reshape(FOLD_WAYS, rows // (8 * FOLD_WAYS), 8, n), axis=1)
    return op(part, axis=0)


def _rope(y, c, s):
    return y * c + pltpu.roll(y, LANES // 2, 1) * s


def _proj_attn_kernel(x_ref, g1_ref, w_ref, wvt_ref, qg_ref, kg_ref, ca_ref, sa_ref, ci_ref, si_ref,
                      q_out, k_out, vt_out, iq_out, ik_out, iw_out):
    h = _rms_rows(x_ref[...], g1_ref[...]).astype(BF16)
    ca, sa = ca_ref[...], sa_ref[...]
    ci, si = ci_ref[...], si_ref[...]

    qscale = HEAD_DIM ** -0.5 * LOG2_E

    def q_head(hd, y):
        y = _rope(_rms_rows(y, qg_ref[...]), ca, sa)
        q_out[:, hd * LANES:(hd + 1) * LANES] = (y * qscale).astype(BF16)

    def k_head(hd, y):
        y = _rope(_rms_rows(y, kg_ref[...]), ca, sa)
        k_out[:, hd * LANES:(hd + 1) * LANES] = y.astype(BF16)

    def iq_head(hd, y):
        iq_out[:, hd * LANES:(hd + 1) * LANES] = _rope(y, ci, si).astype(BF16)

    def ik_tile(_, y):
        ik_out[...] = _rope(y, ci, si).astype(BF16)

    def iw_tile(_, y):
        iw_out[...] = y * (IDX_HEADS ** -0.5 * IDX_DIM ** -0.5)

    tiles = ([(q_head, i) for i in range(N_HEADS)] + [(k_head, i) for i in range(N_KV_HEADS)]
             + [(iq_head, i) for i in range(IDX_HEADS)] + [(ik_tile, 0), (iw_tile, 0)])

    def pair(p):
        return jnp.dot(h, w_ref[:, 2 * p * LANES:(2 * p + 2) * LANES], preferred_element_type=F32)

    nxt = pair(0)
    for p in range(len(tiles) // 2):
        cur = nxt
        if 2 * p + 2 < len(tiles):
            nxt = pair(p + 1)
        for half in range(2):
            fn, arg = tiles[2 * p + half]
            fn(arg, cur[:, half * LANES:(half + 1) * LANES])
    v_t = lax.dot_general(wvt_ref[...], h, NT_DIMS, preferred_element_type=F32)
    tk = vt_out.shape[-1]
    ones_blk = jnp.ones((VT_ROWS - HEAD_DIM, tk), BF16)
    for j in range(vt_out.shape[0]):
        for g in range(N_KV_HEADS):
            vt_out[j, g * VT_ROWS:g * VT_ROWS + HEAD_DIM, :] = (
                v_t[g * HEAD_DIM:(g + 1) * HEAD_DIM, j * tk:(j + 1) * tk].astype(BF16))
            vt_out[j, g * VT_ROWS + HEAD_DIM:(g + 1) * VT_ROWS, :] = ones_blk


def _proj_a_kernel(x_ref, g1_ref, w_ref, a_out):
    h = _rms_rows(x_ref[...], g1_ref[...]).astype(BF16)
    cw = 2 * LANES
    for j in range(D_MODEL // cw):
        val = jnp.dot(h, w_ref[:, j * cw:(j + 1) * cw], preferred_element_type=F32)
        gate = jnp.dot(h, w_ref[:, D_MODEL + j * cw:D_MODEL + (j + 1) * cw], preferred_element_type=F32)
        a_out[:, j * cw:(j + 1) * cw] = (val * jax.nn.sigmoid(gate)).astype(BF16)


def _proj_c_kernel(x_ref, g1_ref, w_ref, cb_out, ccx_out):
    h = _rms_rows(x_ref[...], g1_ref[...]).astype(BF16)
    cw = 2 * LANES
    for j in range(D_MODEL // cw):
        cb = jnp.dot(h, w_ref[:, j * cw:(j + 1) * cw], preferred_element_type=F32)
        cc = jnp.dot(h, w_ref[:, D_MODEL + j * cw:D_MODEL + (j + 1) * cw], preferred_element_type=F32)
        cx = jnp.dot(h, w_ref[:, 2 * D_MODEL + j * cw:2 * D_MODEL + (j + 1) * cw], preferred_element_type=F32)
        cb_out[:, j * cw:(j + 1) * cw] = cb.astype(BF16)
        ccx_out[:, j * cw:(j + 1) * cw] = (cc * cx).astype(BF16)


def _proj_g_kernel(x_ref, g1_ref, w_ref, g_out):
    h = _rms_rows(x_ref[...], g1_ref[...]).astype(BF16)
    cw = 2 * LANES
    for j in range(3 * D_MODEL // cw):
        g = jnp.dot(h, w_ref[:, j * cw:(j + 1) * cw], preferred_element_type=F32)
        g_out[:, j * cw:(j + 1) * cw] = jax.nn.sigmoid(g).astype(BF16)


def _row_spec(tm, n):
    return pl.BlockSpec((tm, n), lambda i: (i, 0))


def _const_spec(shape):
    return pl.BlockSpec(shape, lambda i: (0,) * len(shape))


def _input_projections(x2, g1, w_att, w_vt, w_a, w_c, w_g, qg, kg, tabs_a, tabs_i, seq, tm_attn, tm, tk):
    m = x2.shape[0]
    tpb = seq // tm_attn
    tab_spec = pl.BlockSpec((tm_attn, LANES), lambda i: (i % tpb, 0))
    n_att = w_att.shape[1]
    q, k, vt, iq, ik, iw = pl.pallas_call(
        _proj_attn_kernel,
        grid=(m // tm_attn,),
        in_specs=[_row_spec(tm_attn, D_MODEL), _const_spec((1, D_MODEL)), _const_spec((D_MODEL, n_att)),
                  _const_spec((KV_COLS, D_MODEL)), _const_spec((1, LANES)), _const_spec((1, LANES))]
        + [tab_spec] * 4,
        out_specs=[_row_spec(tm_attn, Q_COLS), _row_spec(tm_attn, KV_COLS),
                   pl.BlockSpec((tm_attn // tk, N_KV_HEADS * VT_ROWS, tk), lambda i: (i, 0, 0)),
                   _row_spec(tm_attn, IDX_HEADS * LANES), _row_spec(tm_attn, LANES), _row_spec(tm_attn, LANES)],
        out_shape=[jax.ShapeDtypeStruct((m, Q_COLS), BF16), jax.ShapeDtypeStruct((m, KV_COLS), BF16),
                   jax.ShapeDtypeStruct((m // tk, N_KV_HEADS * VT_ROWS, tk), BF16),
                   jax.ShapeDtypeStruct((m, IDX_HEADS * LANES), BF16),
                   jax.ShapeDtypeStruct((m, LANES), BF16), jax.ShapeDtypeStruct((m, LANES), F32)],
        compiler_params=_cparams("parallel"),
        name="proj_attn",
    )(x2, g1, w_att, w_vt, qg, kg, *tabs_a, *tabs_i)
    grid = (m // tm,)
    a_act = pl.pallas_call(
        _proj_a_kernel, grid=grid,
        in_specs=[_row_spec(tm, D_MODEL), _const_spec((1, D_MODEL)), _const_spec((D_MODEL, 2 * D_MODEL))],
        out_specs=_row_spec(tm, D_MODEL),
        out_shape=jax.ShapeDtypeStruct((m, D_MODEL), BF16),
        compiler_params=_cparams("parallel"), name="proj_a",
    )(x2, g1, w_a)
    cb, ccx = pl.pallas_call(
        _proj_c_kernel, grid=grid,
        in_specs=[_row_spec(tm, D_MODEL), _const_spec((1, D_MODEL)), _const_spec((D_MODEL, 3 * D_MODEL))],
        out_specs=[_row_spec(tm, D_MODEL), _row_spec(tm, D_MODEL)],
        out_shape=[jax.ShapeDtypeStruct((m, D_MODEL), BF16)] * 2,
        compiler_params=_cparams("parallel"), name="proj_c",
    )(x2, g1, w_c)
    gates = pl.pallas_call(
        _proj_g_kernel, grid=grid,
        in_specs=[_row_spec(tm, D_MODEL), _const_spec((1, D_MODEL)), _const_spec((D_MODEL, 3 * D_MODEL))],
        out_specs=_row_spec(tm, 3 * D_MODEL),
        out_shape=jax.ShapeDtypeStruct((m, 3 * D_MODEL), BF16),
        compiler_params=_cparams("parallel"), name="proj_g",
    )(x2, g1, w_g)
    return q, k, vt, iq, ik, iw, a_act, cb, ccx, gates


def _dsa_kernel(iq_ref, ik_ref, iw_ref, q_ref, k_ref, vt_ref, o_ref, keys_ref, kb_ref, acc_ref,
                bias_ref, lg_ref, pb_ref, *, tq, tk, seq, ksel):
    qi = pl.program_id(1)
    n_it = ((qi + 1) * tq + tk - 1) // tk
    iw_t = iw_ref[0].T
    t_idx = qi * tq + lax.broadcasted_iota(jnp.int32, (tk, tq), 1)
    row_iota = lax.broadcasted_iota(jnp.int32, (tk, tq), 0)

    def chunk_start(c):
        return pl.multiple_of(c * tk, tk)

    def score_chunk(c, carry):
        k0 = chunk_start(c)
        ikc = ik_ref[0, pl.ds(k0, tk), :]
        acc = jnp.zeros((tk, tq), F32)
        for hd in range(IDX_HEADS):
            iqh = iq_ref[0, :, hd * LANES:(hd + 1) * LANES]
            rel = lax.dot_general(ikc, iqh, NT_DIMS, preferred_element_type=F32)
            acc = acc + jnp.maximum(rel, 0.0) * iw_t[hd:hd + 1, :]
        bits = lax.bitcast_convert_type(acc, jnp.int32)
        key = jnp.where(bits < 0, -(bits & 0x7FFFFFFF), bits)
        key = jnp.where(k0 + row_iota <= t_idx, key, INT_MIN)
        keys_ref[pl.ds(k0, tk), :] = key
        return carry

    lax.fori_loop(0, n_it, score_chunk, 0)

    blk_iota = lax.broadcasted_iota(jnp.int32, (COUNT_ROWS, tq), 0)
    kf = float(ksel)
    packed_rows = PACKED_ROWS

    def to_pattern(v):
        return lax.bitcast_convert_type(v << 16, F32).astype(BF16)

    def write_patterns(ph, prefix):
        lo = 32 - sum(DIGIT_BITS[:ph + 1])
        mask = (1 << DIGIT_BITS[ph]) - 1

        def body(c, carry):
            k0 = chunk_start(c)
            for rb in range(tk // COUNT_ROWS):
                rows = pl.ds(pl.multiple_of(k0 + rb * COUNT_ROWS, COUNT_ROWS), COUNT_ROWS)
                kk = keys_ref[rows, :]
                if ph == 0:
                    pat = (((kk >> lo) & mask) ^ (1 << (DIGIT_BITS[0] - 1))) + PATTERN_BASE
                else:
                    pat = jnp.where((kk >> (lo + DIGIT_BITS[ph])) == prefix, ((kk >> lo) & mask) + PATTERN_BASE, 0)
                kb_ref[rows, :] = to_pattern(pat)
            return carry

        lax.fori_loop(0, n_it, body, 0)

    def count_ge(cand):
        cand_b = to_pattern(jnp.broadcast_to(cand + PATTERN_BASE, (packed_rows, tq)))
        groups = tk // packed_rows

        def body(c, acc):
            chunk = kb_ref[pl.ds(chunk_start(c), tk), :]
            parts = [None] * FOLD_WAYS
            for gi in range(groups):
                kb = chunk[gi * packed_rows:(gi + 1) * packed_rows]
                m = jnp.where(kb >= cand_b, jnp.ones_like(kb), jnp.zeros_like(kb))
                w = gi % FOLD_WAYS
                parts[w] = m if parts[w] is None else parts[w] + m
            total = parts[0]
            for w in range(1, FOLD_WAYS):
                total = total + parts[w]
            return acc + total.astype(F32)

        acc = lax.fori_loop(0, n_it, body, jnp.zeros((packed_rows, tq), F32))
        return acc.sum(axis=0, keepdims=True)

    def digit_search(ph, above):
        nbits = DIGIT_BITS[ph]

        def bit_body(i, t):
            cand = t | jnp.left_shift(jnp.int32(1), nbits - 1 - i)
            return jnp.where(above + count_ge(cand) >= kf, cand, t)

        t = lax.fori_loop(0, nbits, bit_body, jnp.zeros((1, tq), jnp.int32))
        return t, above + count_ge(t + 1)

    def count(pred):
        def body(c, acc8):
            k0 = chunk_start(c)
            for rb in range(tk // COUNT_ROWS):
                b0 = k0 + rb * COUNT_ROWS
                kk = keys_ref[pl.ds(pl.multiple_of(b0, COUNT_ROWS), COUNT_ROWS), :]
                m = jnp.where(pred(kk, b0 + blk_iota), 1.0, 0.0)
                acc8 = acc8 + m.reshape(COUNT_ROWS // SUBLANES, SUBLANES, tq).sum(axis=0)
            return acc8
        acc8 = lax.fori_loop(0, n_it, body, jnp.zeros((SUBLANES, tq), F32))
        return acc8.sum(axis=0, keepdims=True)

    write_patterns(0, None)
    t0, above = digit_search(0, jnp.zeros((1, tq), F32))
    prefix = t0 - (1 << (DIGIT_BITS[0] - 1))
    for ph in range(1, len(DIGIT_BITS)):
        write_patterns(ph, prefix)
        t, above = digit_search(ph, above)
        prefix = (prefix << DIGIT_BITS[ph]) | t
    low_bits = 32 - sum(DIGIT_BITS)

    def low_bit_body(i, t):
        cand = t | jnp.left_shift(jnp.int32(1), low_bits - 1 - i)
        cand_key = (prefix << low_bits) | cand
        return jnp.where(count(lambda kk, rows: kk >= cand_key) >= kf, cand, t)

    t_low = lax.fori_loop(0, low_bits, low_bit_body, jnp.zeros((1, tq), jnp.int32))
    thr = (prefix << low_bits) | t_low
    c_ge = count(lambda kk, rows: kk >= thr)
    has_thr = thr != INT_MIN
    excess = jnp.logical_and(c_ge > kf, has_thr)

    @pl.when(jnp.max(jnp.where(excess, 1.0, 0.0)) > 0.0)
    def _():
        nbits = (seq - 1).bit_length()
        need = kf - count(lambda kk, rows: kk > thr)

        def tie_body(i, p):
            cand = p | jnp.left_shift(jnp.int32(1), nbits - 1 - i)
            cnt = count(lambda kk, rows: jnp.logical_and(kk == thr, rows < cand))
            return jnp.where(cnt < need, cand, p)

        p_last = lax.fori_loop(0, nbits, tie_body, jnp.zeros((1, tq), jnp.int32))

        def demote(c, carry):
            k0 = chunk_start(c)
            for rb in range(tk // COUNT_ROWS):
                b0 = k0 + rb * COUNT_ROWS
                rows = pl.ds(pl.multiple_of(b0, COUNT_ROWS), COUNT_ROWS)
                kk = keys_ref[rows, :]
                drop = jnp.logical_and(jnp.logical_and(kk == thr, b0 + blk_iota > p_last), excess)
                keys_ref[rows, :] = jnp.where(drop, kk - 1, kk)
            return carry

        lax.fori_loop(0, n_it, demote, 0)

    thr_sel = jnp.where(has_thr, thr, INT_MIN + 1)

    acc_ref[...] = jnp.zeros(acc_ref.shape, F32)

    def attend(c, stats):
        k0 = chunk_start(c)
        nblk = tk // COUNT_ROWS
        grp = COUNT_ROWS // SUBLANES

        def rows(rb):
            return slice(rb * COUNT_ROWS, (rb + 1) * COUNT_ROWS)

        for rb in range(nblk):
            b0 = k0 + rb * COUNT_ROWS
            kk = keys_ref[pl.ds(pl.multiple_of(b0, COUNT_ROWS), COUNT_ROWS), :]
            bias_ref[rows(rb), :] = jnp.where(kk >= thr_sel, 0.0, MASK_NEG).astype(BF16)
        new_stats = []
        pgrp = COUNT_ROWS // PACKED_ROWS

        def logits(hd):
            g = hd // HEADS_PER_KV
            kc = k_ref[0, pl.ds(k0, tk), g * LANES:(g + 1) * LANES]
            qh = q_ref[0, :, hd * LANES:(hd + 1) * LANES]
            lg = lax.dot_general(kc, qh, NT_DIMS, preferred_element_type=F32)
            cm = None
            for rb in range(nblk):
                blk = lg[rows(rb)].astype(BF16) + bias_ref[rows(rb), :]
                lg_ref[hd % 2, rows(rb), :] = blk
                for gi in range(pgrp):
                    part = blk[gi * PACKED_ROWS:(gi + 1) * PACKED_ROWS]
                    cm = part if cm is None else jnp.maximum(cm, part)
            return cm

        def softmax_pv(hd, cm):
            g = hd // HEADS_PER_KV
            m_old, l_old = stats[hd]
            m_new = jnp.maximum(m_old, cm.astype(F32).max(axis=0, keepdims=True))
            m_b = jnp.broadcast_to(m_new, (PACKED_ROWS, tq)).astype(BF16)
            for rb in range(nblk):
                x = lg_ref[hd % 2, rows(rb), :].reshape(pgrp, PACKED_ROWS, tq)
                pb_ref[hd % 2, rows(rb), :] = jnp.exp2(x - m_b[None]).reshape(COUNT_ROWS, tq)
            alpha = jnp.exp2(m_old - m_new)
            vtc = vt_ref[0, c, g * VT_ROWS:(g + 1) * VT_ROWS, :]
            pv = jnp.dot(vtc, pb_ref[hd % 2], preferred_element_type=F32)
            l_new = alpha * l_old + pv[HEAD_DIM:HEAD_DIM + 1]
            new_stats.append((m_new, l_new))
            return alpha, pv[:HEAD_DIM]

        cms, pvs = {}, {}
        for s in range(N_HEADS + 2):
            if s < N_HEADS:
                cms[s] = logits(s)
            if 1 <= s <= N_HEADS:
                pvs[s - 1] = softmax_pv(s - 1, cms.pop(s - 1))
            if s >= 2:
                alpha, pv = pvs.pop(s - 2)
                acc_ref[s - 2] = alpha * acc_ref[s - 2] + pv
        return tuple(new_stats)

    init = tuple((jnp.full((1, tq), MASK_NEG, F32), jnp.zeros((1, tq), F32)) for _ in range(N_HEADS))
    stats = lax.fori_loop(0, n_it, attend, init)

    for hd in range(N_HEADS):
        out_t = acc_ref[hd] / stats[hd][1]
        o_ref[0, :, hd * LANES:(hd + 1) * LANES] = out_t.T.astype(BF16)


def _sparse_attention(iq, ik, iw, q, k, vt, tq, tk, ksel):
    b, seq, _ = iq.shape
    nq = seq // tq
    nc = seq // tk
    tile = lambda n: pl.BlockSpec((1, tq, n), lambda bi, qi: (bi, qi, 0))
    whole = lambda n: pl.BlockSpec((1, seq, n), lambda bi, qi: (bi, 0, 0))
    return pl.pallas_call(
        functools.partial(_dsa_kernel, tq=tq, tk=tk, seq=seq, ksel=ksel),
        grid=(b, nq),
        in_specs=[tile(IDX_HEADS * LANES), whole(LANES), tile(LANES), tile(Q_COLS), whole(KV_COLS),
                  pl.BlockSpec((1, nc, N_KV_HEADS * VT_ROWS, tk), lambda bi, qi: (bi, 0, 0, 0))],
        out_specs=tile(Q_COLS),
        out_shape=jax.ShapeDtypeStruct((b, seq, Q_COLS), BF16),
        scratch_shapes=[pltpu.VMEM((seq, tq), jnp.int32), pltpu.VMEM((seq, tq), BF16),
                        pltpu.VMEM((N_HEADS, LANES, tq), F32), pltpu.VMEM((tk, tq), BF16),
                        pltpu.VMEM((2, tk, tq), BF16), pltpu.VMEM((2, tk, tq), BF16)],
        compiler_params=_cparams("parallel", "parallel"),
        name="dsa",
    )(iq, ik, iw, q, k, vt)


CONF_HALO = 32
SC_HALO = 16
CONV_ROWS = 64


def _mix_kernel(x_ref, a_ref, ah_ref, ccx_ref, ch_ref, cb_ref, g_ref, o_ref,
                cw_ref, cbias_ref, lng_ref, lnb_ref, wconf_ref, scw_ref, wsc_ref, wattn_ref, wo_ref,
                out_ref, xa_ref, xc_ref, ya_ref, yc_ref, *, tm):
    first = pl.program_id(1) == 0
    ah = ah_ref[0].astype(F32)
    xa_ref[0, 0:CONF_HALO, :] = jnp.where(first, 0.0, ah)
    xa_ref[0, CONF_HALO:, :] = a_ref[0].astype(F32)
    shifted_rows = CONF_HALO + tm - SUBLANES
    for r in range(1, SUBLANES):
        xa_ref[r, 0:shifted_rows, :] = xa_ref[0, r:r + shifted_rows, :]
    ch = ch_ref[0].astype(F32)
    xc_ref[0:SC_HALO, :] = jnp.where(first, 0.0, ch)
    xc_ref[SC_HALO:, :] = ccx_ref[0].astype(F32)

    def conv_lanes(lc, carry):
        ls = pl.ds(pl.multiple_of(lc * LANES, LANES), LANES)
        for rc in range(tm // CONV_ROWS):
            r0 = rc * CONV_ROWS
            acc = jnp.zeros((CONV_ROWS, LANES), F32)
            for j in range(CONF_KERNEL):
                off = CONF_HALO - (CONF_KERNEL - 1) + j
                base = r0 + off - off % SUBLANES
                acc = acc + xa_ref[off % SUBLANES, base:base + CONV_ROWS, ls] * cw_ref[j:j + 1, ls]
            ya_ref[r0:r0 + CONV_ROWS, ls] = acc + cbias_ref[:, ls]
            acc = jnp.zeros((CONV_ROWS, LANES), F32)
            for j in range(SC_KERNEL):
                off = SC_HALO - (SC_KERNEL - 1) + j
                acc = acc + xc_ref[r0 + off:r0 + off + CONV_ROWS, ls] * scw_ref[j:j + 1, ls]
            yc_ref[r0:r0 + CONV_ROWS, ls] = acc
        return carry

    lax.fori_loop(0, D_MODEL // LANES, conv_lanes, 0)

    ya = ya_ref[...]
    mu = jnp.mean(ya, axis=-1, keepdims=True)
    yc0 = ya - mu
    rs = lax.rsqrt(jnp.mean(yc0 * yc0, axis=-1, keepdims=True) + NORM_EPS)
    ln = yc0 * rs * lng_ref[...] + lnb_ref[...]
    act = (ln * jax.nn.sigmoid(ln)).astype(BF16)
    y_conf = jnp.dot(act, wconf_ref[...], preferred_element_type=F32)
    y_sc = jnp.dot((cb_ref[0].astype(F32) * yc_ref[...]).astype(BF16), wsc_ref[...], preferred_element_type=F32)
    y_attn = jnp.dot(o_ref[0], wattn_ref[...], preferred_element_type=F32)
    g = g_ref[0]
    merged = (g[:, 0:D_MODEL].astype(F32) * y_conf + g[:, D_MODEL:2 * D_MODEL].astype(F32) * y_attn
              + g[:, 2 * D_MODEL:3 * D_MODEL].astype(F32) * y_sc)
    out_ref[0] = x_ref[0] + jnp.dot(merged.astype(BF16), wo_ref[...], preferred_element_type=F32)


def _mix(x3, a_act, ccx, cb, gates, o, conf_w, conf_b, ln_g, ln_b, w_conf, sc_w, w_sc, w_attn, w_o, tm):
    b, seq, _ = x3.shape
    nt = seq // tm

    def tile(n):
        return pl.BlockSpec((1, tm, n), lambda bi, ti: (bi, ti, 0))

    def halo(rows):
        per = tm // rows
        return pl.BlockSpec((1, rows, D_MODEL), lambda bi, ti: (bi, jnp.maximum(ti * per - 1, 0), 0))

    def const(shape):
        return pl.BlockSpec(shape, lambda bi, ti: (0,) * len(shape))

    sq = (D_MODEL, D_MODEL)
    return pl.pallas_call(
        functools.partial(_mix_kernel, tm=tm),
        grid=(b, nt),
        in_specs=[tile(D_MODEL), tile(D_MODEL), halo(CONF_HALO), tile(D_MODEL), halo(SC_HALO), tile(D_MODEL),
                  tile(3 * D_MODEL), tile(Q_COLS),
                  const((CONF_KERNEL, D_MODEL)), const((1, D_MODEL)), const((1, D_MODEL)), const((1, D_MODEL)),
                  const(sq), const((SC_KERNEL, D_MODEL)), const(sq), const(sq), const(sq)],
        out_specs=tile(D_MODEL),
        out_shape=jax.ShapeDtypeStruct((b, seq, D_MODEL), F32),
        scratch_shapes=[pltpu.VMEM((SUBLANES, CONF_HALO + tm, D_MODEL), F32),
                        pltpu.VMEM((SC_HALO + tm, D_MODEL), F32),
                        pltpu.VMEM((tm, D_MODEL), F32), pltpu.VMEM((tm, D_MODEL), F32)],
        compiler_params=_cparams("parallel", "parallel"),
        name="mix",
    )(x3, a_act, a_act, ccx, ccx, cb, gates, o, conf_w, conf_b, ln_g, ln_b, w_conf, sc_w, w_sc, w_attn, w_o)


FFN_CHUNK = 256
FFN_CARRY = 8


def _ffn_kernel(x_ref, g2_ref, wg_ref, wv_ref, cwg_ref, cwv_ref, cbg_ref, cbv_ref, wd_ref, out_ref,
                h_ref, ug_ref, uv_ref, act_ref, carry_g, carry_v, *, tm, tiles_per_seq):
    first = (pl.program_id(0) % tiles_per_seq) == 0
    x = x_ref[...]
    h_ref[...] = _rms_rows(x, g2_ref[...]).astype(BF16)

    nf = FFN_DIM // FFN_CHUNK

    def chunk_cols(f):
        return pl.ds(pl.multiple_of(f * FFN_CHUNK, FFN_CHUNK), FFN_CHUNK)

    def up_stage(f, slot):
        cols = chunk_cols(f)
        h = h_ref[...]
        ug_ref[slot, FFN_CARRY:, :] = jnp.dot(h, wg_ref[:, cols], preferred_element_type=F32)
        uv_ref[slot, FFN_CARRY:, :] = jnp.dot(h, wv_ref[:, cols], preferred_element_type=F32)

    def conv_branch(f, slot, cols, u_ref, carry_ref, cw_ref, cb_ref):
        u_ref[slot, 0:FFN_CARRY, :] = jnp.where(first, 0.0, carry_ref[f])
        carry_ref[f] = u_ref[slot, tm:tm + FFN_CARRY, :]
        out = cb_ref[:, cols]
        for j in range(FFN_KERNEL):
            off = FFN_CARRY - (FFN_KERNEL - 1) + j
            out = out + u_ref[slot, pl.ds(off, tm), :] * cw_ref[j:j + 1, cols]
        return out

    def gate_stage(f, slot):
        cols = chunk_cols(f)
        ug = conv_branch(f, slot, cols, ug_ref, carry_g, cwg_ref, cbg_ref)
        uv = conv_branch(f, slot, cols, uv_ref, carry_v, cwv_ref, cbv_ref)
        act_ref[:, cols] = (ug * jax.nn.sigmoid(ug) * uv).astype(BF16)

    up_stage(0, 0)

    def chunk_pair(i, carry):
        f = 2 * i
        up_stage(f + 1, 1)
        gate_stage(f, 0)
        up_stage(f + 2, 0)
        gate_stage(f + 1, 1)
        return carry

    assert nf % 2 == 1
    lax.fori_loop(0, nf // 2, chunk_pair, 0)
    gate_stage(nf - 1, 0)
    out_ref[...] = x + jnp.dot(act_ref[...], wd_ref[...], preferred_element_type=F32)


def _ffn(x2, g2, w_up_g, w_up_v, cw_g, cw_v, cb_g, cb_v, w_down, seq, tm):
    m = x2.shape[0]
    nf = FFN_DIM // FFN_CHUNK

    def const(shape):
        return pl.BlockSpec(shape, lambda mi: (0, 0), pipeline_mode=pl.Buffered(1))

    return pl.pallas_call(
        functools.partial(_ffn_kernel, tm=tm, tiles_per_seq=seq // tm),
        grid=(m // tm,),
        in_specs=[pl.BlockSpec((tm, D_MODEL), lambda mi: (mi, 0)),
                  const((1, D_MODEL)),
                  const((D_MODEL, FFN_DIM)), const((D_MODEL, FFN_DIM)),
                  const((FFN_KERNEL, FFN_DIM)), const((FFN_KERNEL, FFN_DIM)),
                  const((1, FFN_DIM)), const((1, FFN_DIM)),
                  const((FFN_DIM, D_MODEL))],
        out_specs=pl.BlockSpec((tm, D_MODEL), lambda mi: (mi, 0)),
        out_shape=jax.ShapeDtypeStruct((m, D_MODEL), F32),
        scratch_shapes=[pltpu.VMEM((tm, D_MODEL), BF16),
                        pltpu.VMEM((2, FFN_CARRY + tm, FFN_CHUNK), F32),
                        pltpu.VMEM((2, FFN_CARRY + tm, FFN_CHUNK), F32),
                        pltpu.VMEM((tm, FFN_DIM), BF16),
                        pltpu.VMEM((nf, FFN_CARRY, FFN_CHUNK), F32), pltpu.VMEM((nf, FFN_CARRY, FFN_CHUNK), F32)],
        compiler_params=_cparams("arbitrary"),
        name="ffn",
    )(x2, g2, w_up_g, w_up_v, cw_g, cw_v, cb_g, cb_v, w_down)


def _rope_layout(dim, rot_dim):
    half = rot_dim // 2
    rest = list(range(rot_dim, dim))
    lo_fill = LANES // 2 - half
    lanes = list(range(half)) + rest[:lo_fill] + [-1] * max(0, lo_fill - len(rest))
    lanes += list(range(half, rot_dim)) + rest[lo_fill:]
    return lanes + [-1] * (LANES - len(lanes))


def _to_layout(w, layout):
    idx = jnp.asarray([max(i, 0) for i in layout])
    keep = jnp.asarray([1.0 if i >= 0 else 0.0 for i in layout], w.dtype)
    return jnp.take(w, idx, axis=-1) * keep


def _rope_lane_tables(seq, rot_dim):
    half = rot_dim // 2
    pos = jnp.arange(seq, dtype=F32)
    inv_freq = jnp.power(ROPE_THETA, -jnp.arange(0, rot_dim, 2, dtype=F32) / rot_dim)
    ang = pos[:, None] * inv_freq[None, :]
    cos, sin = jnp.cos(ang), jnp.sin(ang)
    one = jnp.ones((seq, LANES // 2 - half), F32)
    zero = jnp.zeros((seq, LANES // 2 - half), F32)
    c = jnp.concatenate([cos, one, cos, one], axis=1)
    s = jnp.concatenate([-sin, zero, sin, zero], axis=1)
    return c, s


def _pad_cols(w, n):
    return jnp.pad(w, ((0, 0), (0, 0), (0, n - w.shape[-1])))


def _tile_rows(seq, target):
    t = min(target, seq)
    assert seq % t == 0
    return t


def kernel(x, norm1_g, w_in, q_norm_g, k_norm_g, w_attn_out, conf_conv_w, conf_conv_b, conf_ln_g, conf_ln_b,
           w_conf_out, sc_conv_w, w_sc_out, w_o, norm2_g, w_up, ffn_conv_w, ffn_conv_b, w_down):
    b, seq, d = x.shape
    depth = w_in.shape[0]
    assert d == D_MODEL and seq % LANES == 0
    tq = _tile_rows(seq, 256)
    tk = _tile_rows(seq, 512)
    tm_proj_attn = _tile_rows(seq, 512)
    tm_proj = _tile_rows(seq, 1024)
    tm_mix = _tile_rows(seq, 256)
    tm_ffn = _tile_rows(seq, 1024)
    ksel = min(TOPK_MAX, seq // 4)

    rot_a, rot_i = HEAD_DIM // ROPE_FRACTION_DIV, IDX_DIM // ROPE_FRACTION_DIV
    tabs_a = _rope_lane_tables(seq, rot_a)
    tabs_i = _rope_lane_tables(seq, rot_i)
    lay_a, lay_i = _rope_layout(HEAD_DIM, rot_a), _rope_layout(IDX_DIM, rot_i)

    def heads_to_layout(w, n_heads, dim, layout):
        return _to_layout(w.reshape(w.shape[:-1] + (n_heads, dim)), layout).reshape(w.shape[:-1] + (n_heads * LANES,))

    offs = [0]
    for s in IN_SIZES:
        offs.append(offs[-1] + s)
    wq, wk, wv, wiq, wik, wiw, wa, wc, wg = (w_in[:, :, offs[i]:offs[i + 1]] for i in range(len(IN_SIZES)))
    w_att = jnp.concatenate([heads_to_layout(wq, N_HEADS, HEAD_DIM, lay_a),
                             heads_to_layout(wk, N_KV_HEADS, HEAD_DIM, lay_a),
                             heads_to_layout(wiq, IDX_HEADS, IDX_DIM, lay_i),
                             _to_layout(wik, lay_i), _pad_cols(wiw, LANES)], axis=-1).astype(BF16)
    layers = dict(
        g1=norm1_g[:, None, :], w_att=w_att, w_vt=jnp.swapaxes(wv, 1, 2).astype(BF16),
        w_a=wa.astype(BF16), w_c=wc.astype(BF16), w_g=wg.astype(BF16),
        qg=_to_layout(q_norm_g, lay_a)[:, None, :], kg=_to_layout(k_norm_g, lay_a)[:, None, :],
        conf_w=conf_conv_w, conf_b=conf_conv_b[:, None, :], ln_g=conf_ln_g[:, None, :], ln_b=conf_ln_b[:, None, :],
        w_conf=w_conf_out.astype(BF16), sc_w=sc_conv_w, w_sc=w_sc_out.astype(BF16),
        w_attn=w_attn_out.astype(BF16), w_o=w_o.astype(BF16),
        g2=norm2_g[:, None, :],
        w_up_g=w_up[:, :, :FFN_DIM].astype(BF16), w_up_v=w_up[:, :, FFN_DIM:].astype(BF16),
        cw_g=ffn_conv_w[:, :, :FFN_DIM], cw_v=ffn_conv_w[:, :, FFN_DIM:],
        cb_g=ffn_conv_b[:, None, :FFN_DIM], cb_v=ffn_conv_b[:, None, FFN_DIM:],
        w_down=w_down.astype(BF16),
    )

    def layer(xc, p):
        x2 = xc.reshape(b * seq, d)
        q, k, vt, iq, ik, iw, a_act, cb, ccx, gates = _input_projections(
            x2, p["g1"], p["w_att"], p["w_vt"], p["w_a"], p["w_c"], p["w_g"], p["qg"], p["kg"],
            tabs_a, tabs_i, seq, tm_proj_attn, tm_proj, tk)
        r3 = lambda t: t.reshape(b, seq, t.shape[-1])
        o = _sparse_attention(r3(iq), r3(ik), r3(iw), r3(q), r3(k), vt.reshape(b, seq // tk, N_KV_HEADS * VT_ROWS, tk),
                              tq, tk, ksel)
        x1 = _mix(xc, r3(a_act), r3(ccx), r3(cb), r3(gates), o, p["conf_w"], p["conf_b"], p["ln_g"], p["ln_b"],
                  p["w_conf"], p["sc_w"], p["w_sc"], p["w_attn"], p["w_o"], tm_mix)
        x_out = _ffn(x1.reshape(b * seq, d), p["g2"], p["w_up_g"], p["w_up_v"], p["cw_g"], p["cw_v"],
                     p["cb_g"], p["cb_v"], p["w_down"], seq, tm_ffn)
        return x_out.reshape(b, seq, d), None

    out, _ = lax.scan(layer, x, layers)
    return out
```

```python
import functools

import jax
import jax.numpy as jnp
from jax import lax
from jax.experimental import pallas as pl
from jax.experimental.pallas import tpu as pltpu

D_MODEL = 1024
N_HEADS = 8
HEAD_DIM = 128
N_KV_HEADS = 2
HEADS_PER_KV = N_HEADS // N_KV_HEADS
ROPE_FRACTION_DIV = 4
ROPE_THETA = 500000.0
IDX_HEADS = 8
IDX_DIM = 64
TOPK_MAX = 256
CONF_KERNEL = 31
SC_KERNEL = 3
FFN_DIM = 2816
FFN_KERNEL = 3
NORM_EPS = 1e-6

Q_COLS = N_HEADS * HEAD_DIM
KV_COLS = N_KV_HEADS * HEAD_DIM
IQ_COLS = IDX_HEADS * IDX_DIM
IN_SIZES = (Q_COLS, KV_COLS, KV_COLS, IQ_COLS, IDX_DIM, IDX_HEADS, 2 * D_MODEL, 3 * D_MODEL, 3 * D_MODEL)

LANES = 128
SUBLANES = 8
VMEM_LIMIT_BYTES = 56 * 1024 * 1024
MASK_NEG = -1e30
INT_MIN = -(2 ** 31)
LOG2_E = 1.4426950408889634

F32 = jnp.float32
BF16 = jnp.bfloat16
NT_DIMS = (((1,), (1,)), ((), ()))


def _cparams(*sem):
    return pltpu.CompilerParams(dimension_semantics=sem, vmem_limit_bytes=VMEM_LIMIT_BYTES)


def _rms_rows(x, g):
    ms = jnp.mean(x * x, axis=-1, keepdims=True)
    return x * lax.rsqrt(ms + NORM_EPS) * g


FOLD_WAYS = 4
COUNT_ROWS = 64
DIGIT_BITS = (14, 14)
PATTERN_BASE = 128
PACKED_ROWS = 2 * SUBLANES
VT_ROWS = HEAD_DIM + PACKED_ROWS


def _fold_rows(x, op):
    rows, n = x.shape
    part = op(x.reshape(FOLD_WAYS, rows // (8 * FOLD_WAYS), 8, n), axis=1)
    return op(part, axis=0)


def _rope(y, c, s):
    return y * c + pltpu.roll(y, LANES // 2, 1) * s


def _proj_attn_kernel(x_ref, g1_ref, w_ref, wvt_ref, qg_ref, kg_ref, ca_ref, sa_ref, ci_ref, si_ref,
                      q_out, k_out, vt_out, iq_out, ik_out, iw_out):
    h = _rms_rows(x_ref[...], g1_ref[...]).astype(BF16)
    ca, sa = ca_ref[...], sa_ref[...]
    ci, si = ci_ref[...], si_ref[...]

    qscale = HEAD_DIM ** -0.5 * LOG2_E

    def q_head(hd, y):
        y = _rope(_rms_rows(y, qg_ref[...]), ca, sa)
        q_out[:, hd * LANES:(hd + 1) * LANES] = (y * qscale).astype(BF16)

    def k_head(hd, y):
        y = _rope(_rms_rows(y, kg_ref[...]), ca, sa)
        k_out[:, hd * LANES:(hd + 1) * LANES] = y.astype(BF16)

    def iq_head(hd, y):
        iq_out[:, hd * LANES:(hd + 1) * LANES] = _rope(y, ci, si).astype(BF16)

    def ik_tile(_, y):
        ik_out[...] = _rope(y, ci, si).astype(BF16)

    def iw_tile(_, y):
        iw_out[...] = y * (IDX_HEADS ** -0.5 * IDX_DIM ** -0.5)

    tiles = ([(q_head, i) for i in range(N_HEADS)] + [(k_head, i) for i in range(N_KV_HEADS)]
             + [(iq_head, i) for i in range(IDX_HEADS)] + [(ik_tile, 0), (iw_tile, 0)])

    def pair(p):
        return jnp.dot(h, w_ref[:, 2 * p * LANES:(2 * p + 2) * LANES], preferred_element_type=F32)

    nxt = pair(0)
    for p in range(len(tiles) // 2):
        cur = nxt
        if 2 * p + 2 < len(tiles):
            nxt = pair(p + 1)
        for half in range(2):
            fn, arg = tiles[2 * p + half]
            fn(arg, cur[:, half * LANES:(half + 1) * LANES])
    v_t = lax.dot_general(wvt_ref[...], h, NT_DIMS, preferred_element_type=F32)
    tk = vt_out.shape[-1]
    ones_blk = jnp.ones((VT_ROWS - HEAD_DIM, tk), BF16)
    for j in range(vt_out.shape[0]):
        for g in range(N_KV_HEADS):
            vt_out[j, g * VT_ROWS:g * VT_ROWS + HEAD_DIM, :] = (
                v_t[g * HEAD_DIM:(g + 1) * HEAD_DIM, j * tk:(j + 1) * tk].astype(BF16))
            vt_out[j, g * VT_ROWS + HEAD_DIM:(g + 1) * VT_ROWS, :] = ones_blk


def _proj_a_kernel(x_ref, g1_ref, w_ref, a_out):
    h = _rms_rows(x_ref[...], g1_ref[...]).astype(BF16)
    cw = 2 * LANES
    for j in range(D_MODEL // cw):
        val = jnp.dot(h, w_ref[:, j * cw:(j + 1) * cw], preferred_element_type=F32)
        gate = jnp.dot(h, w_ref[:, D_MODEL + j * cw:D_MODEL + (j + 1) * cw], preferred_element_type=F32)
        a_out[:, j * cw:(j + 1) * cw] = (val * jax.nn.sigmoid(gate)).astype(BF16)


def _proj_c_kernel(x_ref, g1_ref, w_ref, cb_out, ccx_out):
    h = _rms_rows(x_ref[...], g1_ref[...]).astype(BF16)
    cw = 2 * LANES
    for j in range(D_MODEL // cw):
        cb = jnp.dot(h, w_ref[:, j * cw:(j + 1) * cw], preferred_element_type=F32)
        cc = jnp.dot(h, w_ref[:, D_MODEL + j * cw:D_MODEL + (j + 1) * cw], preferred_element_type=F32)
        cx = jnp.dot(h, w_ref[:, 2 * D_MODEL + j * cw:2 * D_MODEL + (j + 1) * cw], preferred_element_type=F32)
        cb_out[:, j * cw:(j + 1) * cw] = cb.astype(BF16)
        ccx_out[:, j * cw:(j + 1) * cw] = (cc * cx).astype(BF16)


def _proj_g_kernel(x_ref, g1_ref, w_ref, g_out):
    h = _rms_rows(x_ref[...], g1_ref[...]).astype(BF16)
    cw = 2 * LANES
    for j in range(3 * D_MODEL // cw):
        g = jnp.dot(h, w_ref[:, j * cw:(j + 1) * cw], preferred_element_type=F32)
        g_out[:, j * cw:(j + 1) * cw] = jax.nn.sigmoid(g).astype(BF16)


def _row_spec(tm, n):
    return pl.BlockSpec((tm, n), lambda i: (i, 0))


def _const_spec(shape):
    return pl.BlockSpec(shape, lambda i: (0,) * len(shape))


def _input_projections(x2, g1, w_att, w_vt, w_a, w_c, w_g, qg, kg, tabs_a, tabs_i, seq, tm_attn, tm, tk):
    m = x2.shape[0]
    tpb = seq // tm_attn
    tab_spec = pl.BlockSpec((tm_attn, LANES), lambda i: (i % tpb, 0))
    n_att = w_att.shape[1]
    q, k, vt, iq, ik, iw = pl.pallas_call(
        _proj_attn_kernel,
        grid=(m // tm_attn,),
        in_specs=[_row_spec(tm_attn, D_MODEL), _const_spec((1, D_MODEL)), _const_spec((D_MODEL, n_att)),
                  _const_spec((KV_COLS, D_MODEL)), _const_spec((1, LANES)), _const_spec((1, LANES))]
        + [tab_spec] * 4,
        out_specs=[_row_spec(tm_attn, Q_COLS), _row_spec(tm_attn, KV_COLS),
                   pl.BlockSpec((tm_attn // tk, N_KV_HEADS * VT_ROWS, tk), lambda i: (i, 0, 0)),
                   _row_spec(tm_attn, IDX_HEADS * LANES), _row_spec(tm_attn, LANES), _row_spec(tm_attn, LANES)],
        out_shape=[jax.ShapeDtypeStruct((m, Q_COLS), BF16), jax.ShapeDtypeStruct((m, KV_COLS), BF16),
                   jax.ShapeDtypeStruct((m // tk, N_KV_HEADS * VT_ROWS, tk), BF16),
                   jax.ShapeDtypeStruct((m, IDX_HEADS * LANES), BF16),
                   jax.ShapeDtypeStruct((m, LANES), BF16), jax.ShapeDtypeStruct((m, LANES), F32)],
        compiler_params=_cparams("parallel"),
        name="proj_attn",
    )(x2, g1, w_att, w_vt, qg, kg, *tabs_a, *tabs_i)
    grid = (m // tm,)
    a_act = pl.pallas_call(
        _proj_a_kernel, grid=grid,
        in_specs=[_row_spec(tm, D_MODEL), _const_spec((1, D_MODEL)), _const_spec((D_MODEL, 2 * D_MODEL))],
        out_specs=_row_spec(tm, D_MODEL),
        out_shape=jax.ShapeDtypeStruct((m, D_MODEL), BF16),
        compiler_params=_cparams("parallel"), name="proj_a",
    )(x2, g1, w_a)
    cb, ccx = pl.pallas_call(
        _proj_c_kernel, grid=grid,
        in_specs=[_row_spec(tm, D_MODEL), _const_spec((1, D_MODEL)), _const_spec((D_MODEL, 3 * D_MODEL))],
        out_specs=[_row_spec(tm, D_MODEL), _row_spec(tm, D_MODEL)],
        out_shape=[jax.ShapeDtypeStruct((m, D_MODEL), BF16)] * 2,
        compiler_params=_cparams("parallel"), name="proj_c",
    )(x2, g1, w_c)
    gates = pl.pallas_call(
        _proj_g_kernel, grid=grid,
        in_specs=[_row_spec(tm, D_MODEL), _const_spec((1, D_MODEL)), _const_spec((D_MODEL, 3 * D_MODEL))],
        out_specs=_row_spec(tm, 3 * D_MODEL),
        out_shape=jax.ShapeDtypeStruct((m, 3 * D_MODEL), BF16),
        compiler_params=_cparams("parallel"), name="proj_g",
    )(x2, g1, w_g)
    return q, k, vt, iq, ik, iw, a_act, cb, ccx, gates


def _dsa_kernel(iq_ref, ik_ref, iw_ref, q_ref, k_ref, vt_ref, o_ref, keys_ref, kb_ref, acc_ref,
                bias_ref, lg_ref, pb_ref, *, tq, tk, seq, ksel):
    qi = pl.program_id(1)
    n_it = ((qi + 1) * tq + tk - 1) // tk
    iw_t = iw_ref[0].T
    t_idx = qi * tq + lax.broadcasted_iota(jnp.int32, (tk, tq), 1)
    row_iota = lax.broadcasted_iota(jnp.int32, (tk, tq), 0)

    def chunk_start(c):
        return pl.multiple_of(c * tk, tk)

    def score_chunk(c, carry):
        k0 = chunk_start(c)
        ikc = ik_ref[0, pl.ds(k0, tk), :]
        acc = jnp.zeros((tk, tq), F32)
        for hd in range(IDX_HEADS):
            iqh = iq_ref[0, :, hd * LANES:(hd + 1) * LANES]
            rel = lax.dot_general(ikc, iqh, NT_DIMS, preferred_element_type=F32)
            acc = acc + jnp.maximum(rel, 0.0) * iw_t[hd:hd + 1, :]
        bits = lax.bitcast_convert_type(acc, jnp.int32)
        key = jnp.where(bits < 0, -(bits & 0x7FFFFFFF), bits)
        key = jnp.where(k0 + row_iota <= t_idx, key, INT_MIN)
        keys_ref[pl.ds(k0, tk), :] = key
        return carry

    lax.fori_loop(0, n_it, score_chunk, 0)

    blk_iota = lax.broadcasted_iota(jnp.int32, (COUNT_ROWS, tq), 0)
    kf = float(ksel)
    packed_rows = PACKED_ROWS

    def to_pattern(v):
        return lax.bitcast_convert_type(v << 16, F32).astype(BF16)

    def write_patterns(ph, prefix):
        lo = 32 - sum(DIGIT_BITS[:ph + 1])
        mask = (1 << DIGIT_BITS[ph]) - 1

        def body(c, carry):
            k0 = chunk_start(c)
            for rb in range(tk // COUNT_ROWS):
                rows = pl.ds(pl.multiple_of(k0 + rb * COUNT_ROWS, COUNT_ROWS), COUNT_ROWS)
                kk = keys_ref[rows, :]
                if ph == 0:
                    pat = (((kk >> lo) & mask) ^ (1 << (DIGIT_BITS[0] - 1))) + PATTERN_BASE
                else:
                    pat = jnp.where((kk >> (lo + DIGIT_BITS[ph])) == prefix, ((kk >> lo) & mask) + PATTERN_BASE, 0)
                kb_ref[rows, :] = to_pattern(pat)
            return carry

        lax.fori_loop(0, n_it, body, 0)

    def count_ge(cand):
        cand_b = to_pattern(jnp.broadcast_to(cand + PATTERN_BASE, (packed_rows, tq)))
        groups = tk // packed_rows

        def body(c, acc):
            chunk = kb_ref[pl.ds(chunk_start(c), tk), :]
            parts = [None] * FOLD_WAYS
            for gi in range(groups):
                kb = chunk[gi * packed_rows:(gi + 1) * packed_rows]
                m = jnp.where(kb >= cand_b, jnp.ones_like(kb), jnp.zeros_like(kb))
                w = gi % FOLD_WAYS
                parts[w] = m if parts[w] is None else parts[w] + m
            total = parts[0]
            for w in range(1, FOLD_WAYS):
                total = total + parts[w]
            return acc + total.astype(F32)

        acc = lax.fori_loop(0, n_it, body, jnp.zeros((packed_rows, tq), F32))
        return acc.sum(axis=0, keepdims=True)

    def digit_search(ph, above):
        nbits = DIGIT_BITS[ph]

        def bit_body(i, t):
            cand = t | jnp.left_shift(jnp.int32(1), nbits - 1 - i)
            return jnp.where(above + count_ge(cand) >= kf, cand, t)

        t = lax.fori_loop(0, nbits, bit_body, jnp.zeros((1, tq), jnp.int32))
        return t, above + count_ge(t + 1)

    def count(pred):
        def body(c, acc8):
            k0 = chunk_start(c)
            for rb in range(tk // COUNT_ROWS):
                b0 = k0 + rb * COUNT_ROWS
                kk = keys_ref[pl.ds(pl.multiple_of(b0, COUNT_ROWS), COUNT_ROWS), :]
                m = jnp.where(pred(kk, b0 + blk_iota), 1.0, 0.0)
                acc8 = acc8 + m.reshape(COUNT_ROWS // SUBLANES, SUBLANES, tq).sum(axis=0)
            return acc8
        acc8 = lax.fori_loop(0, n_it, body, jnp.zeros((SUBLANES, tq), F32))
        return acc8.sum(axis=0, keepdims=True)

    write_patterns(0, None)
    t0, above = digit_search(0, jnp.zeros((1, tq), F32))
    prefix = t0 - (1 << (DIGIT_BITS[0] - 1))
    for ph in range(1, len(DIGIT_BITS)):
        write_patterns(ph, prefix)
        t, above = digit_search(ph, above)
        prefix = (prefix << DIGIT_BITS[ph]) | t
    low_bits = 32 - sum(DIGIT_BITS)

    def low_bit_body(i, t):
        cand = t | jnp.left_shift(jnp.int32(1), low_bits - 1 - i)
        cand_key = (prefix << low_bits) | cand
        return jnp.where(count(lambda kk, rows: kk >= cand_key) >= kf, cand, t)

    t_low = lax.fori_loop(0, low_bits, low_bit_body, jnp.zeros((1, tq), jnp.int32))
    thr = (prefix << low_bits) | t_low
    c_ge = count(lambda kk, rows: kk >= thr)
    has_thr = thr != INT_MIN
    excess = jnp.logical_and(c_ge > kf, has_thr)

    @pl.when(jnp.max(jnp.where(excess, 1.0, 0.0)) > 0.0)
    def _():
        nbits = (seq - 1).bit_length()
        need = kf - count(lambda kk, rows: kk > thr)

        def tie_body(i, p):
            cand = p | jnp.left_shift(jnp.int32(1), nbits - 1 - i)
            cnt = count(lambda kk, rows: jnp.logical_and(kk == thr, rows < cand))
            return jnp.where(cnt < need, cand, p)

        p_last = lax.fori_loop(0, nbits, tie_body, jnp.zeros((1, tq), jnp.int32))

        def demote(c, carry):
            k0 = chunk_start(c)
            for rb in range(tk // COUNT_ROWS):
                b0 = k0 + rb * COUNT_ROWS
                rows = pl.ds(pl.multiple_of(b0, COUNT_ROWS), COUNT_ROWS)
                kk = keys_ref[rows, :]
                drop = jnp.logical_and(jnp.logical_and(kk == thr, b0 + blk_iota > p_last), excess)
                keys_ref[rows, :] = jnp.where(drop, kk - 1, kk)
            return carry

        lax.fori_loop(0, n_it, demote, 0)

    thr_sel = jnp.where(has_thr, thr, INT_MIN + 1)

    acc_ref[...] = jnp.zeros(acc_ref.shape, F32)

    def attend(c, stats):
        k0 = chunk_start(c)
        nblk = tk // COUNT_ROWS
        grp = COUNT_ROWS // SUBLANES

        def rows(rb):
            return slice(rb * COUNT_ROWS, (rb + 1) * COUNT_ROWS)

        for rb in range(nblk):
            b0 = k0 + rb * COUNT_ROWS
            kk = keys_ref[pl.ds(pl.multiple_of(b0, COUNT_ROWS), COUNT_ROWS), :]
            bias_ref[rows(rb), :] = jnp.where(kk >= thr_sel, 0.0, MASK_NEG).astype(BF16)
        new_stats = []
        pgrp = COUNT_ROWS // PACKED_ROWS

        def logits(hd):
            g = hd // HEADS_PER_KV
            kc = k_ref[0, pl.ds(k0, tk), g * LANES:(g + 1) * LANES]
            qh = q_ref[0, :, hd * LANES:(hd + 1) * LANES]
            lg = lax.dot_general(kc, qh, NT_DIMS, preferred_element_type=F32)
            cm = None
            for rb in range(nblk):
                blk = lg[rows(rb)].astype(BF16) + bias_ref[rows(rb), :]
                lg_ref[hd % 2, rows(rb), :] = blk
                for gi in range(pgrp):
                    part = blk[gi * PACKED_ROWS:(gi + 1) * PACKED_ROWS]
                    cm = part if cm is None else jnp.maximum(cm, part)
            return cm

        def softmax_pv(hd, cm):
            g = hd // HEADS_PER_KV
            m_old, l_old = stats[hd]
            m_new = jnp.maximum(m_old, cm.astype(F32).max(axis=0, keepdims=True))
            m_b = jnp.broadcast_to(m_new, (PACKED_ROWS, tq)).astype(BF16)
            for rb in range(nblk):
                x = lg_ref[hd % 2, rows(rb), :].reshape(pgrp, PACKED_ROWS, tq)
                pb_ref[hd % 2, rows(rb), :] = jnp.exp2(x - m_b[None]).reshape(COUNT_ROWS, tq)
            alpha = jnp.exp2(m_old - m_new)
            vtc = vt_ref[0, c, g * VT_ROWS:(g + 1) * VT_ROWS, :]
            pv = jnp.dot(vtc, pb_ref[hd % 2], preferred_element_type=F32)
            l_new = alpha * l_old + pv[HEAD_DIM:HEAD_DIM + 1]
            new_stats.append((m_new, l_new))
            return alpha, pv[:HEAD_DIM]

        cms, pvs = {}, {}
        for s in range(N_HEADS + 2):
            if s < N_HEADS:
                cms[s] = logits(s)
            if 1 <= s <= N_HEADS:
                pvs[s - 1] = softmax_pv(s - 1, cms.pop(s - 1))
            if s >= 2:
                alpha, pv = pvs.pop(s - 2)
                acc_ref[s - 2] = alpha * acc_ref[s - 2] + pv
        return tuple(new_stats)

    init = tuple((jnp.full((1, tq), MASK_NEG, F32), jnp.zeros((1, tq), F32)) for _ in range(N_HEADS))
    stats = lax.fori_loop(0, n_it, attend, init)

    for hd in range(N_HEADS):
        out_t = acc_ref[hd] / stats[hd][1]
        o_ref[0, :, hd * LANES:(hd + 1) * LANES] = out_t.T.astype(BF16)


def _sparse_attention(iq, ik, iw, q, k, vt, tq, tk, ksel):
    b, seq, _ = iq.shape
    nq = seq // tq
    nc = seq // tk
    tile = lambda n: pl.BlockSpec((1, tq, n), lambda bi, qi: (bi, qi, 0))
    whole = lambda n: pl.BlockSpec((1, seq, n), lambda bi, qi: (bi, 0, 0))
    return pl.pallas_call(
        functools.partial(_dsa_kernel, tq=tq, tk=tk, seq=seq, ksel=ksel),
        grid=(b, nq),
        in_specs=[tile(IDX_HEADS * LANES), whole(LANES), tile(LANES), tile(Q_COLS), whole(KV_COLS),
                  pl.BlockSpec((1, nc, N_KV_HEADS * VT_ROWS, tk), lambda bi, qi: (bi, 0, 0, 0))],
        out_specs=tile(Q_COLS),
        out_shape=jax.ShapeDtypeStruct((b, seq, Q_COLS), BF16),
        scratch_shapes=[pltpu.VMEM((seq, tq), jnp.int32), pltpu.VMEM((seq, tq), BF16),
                        pltpu.VMEM((N_HEADS, LANES, tq), F32), pltpu.VMEM((tk, tq), BF16),
                        pltpu.VMEM((2, tk, tq), BF16), pltpu.VMEM((2, tk, tq), BF16)],
        compiler_params=_cparams("parallel", "parallel"),
        name="dsa",
    )(iq, ik, iw, q, k, vt)


CONF_HALO = 32
SC_HALO = 16
CONV_ROWS = 64


def _mix_kernel(x_ref, a_ref, ah_ref, ccx_ref, ch_ref, cb_ref, g_ref, o_ref,
                cw_ref, cbias_ref, lng_ref, lnb_ref, wconf_ref, scw_ref, wsc_ref, wattn_ref, wo_ref,
                out_ref, xa_ref, xc_ref, ya_ref, yc_ref, *, tm):
    first = pl.program_id(1) == 0
    ch = ch_ref[0].astype(F32)
    xc_ref[0:SC_HALO, :] = jnp.where(first, 0.0, ch)
    xc_ref[SC_HALO:, :] = ccx_ref[0].astype(F32)
    shifted_rows = CONF_HALO + tm - SUBLANES
    half_w = xa_ref.shape[-1]

    for c0 in range(0, D_MODEL, half_w):
        xa_ref[0, 0:CONF_HALO, :] = jnp.where(first, 0.0, ah_ref[0, :, c0:c0 + half_w].astype(F32))
        xa_ref[0, CONF_HALO:, :] = a_ref[0, :, c0:c0 + half_w].astype(F32)
        for r in range(1, SUBLANES):
            xa_ref[r, 0:shifted_rows, :] = xa_ref[0, r:r + shifted_rows, :]

        def conv_lanes(lc, carry, c0=c0):
            ll = pl.ds(pl.multiple_of(lc * LANES, LANES), LANES)
            ls = pl.ds(pl.multiple_of(c0 + lc * LANES, LANES), LANES)
            for rc in range(tm // CONV_ROWS):
                r0 = rc * CONV_ROWS
                acc = jnp.zeros((CONV_ROWS, LANES), F32)
                for j in range(CONF_KERNEL):
                    off = CONF_HALO - (CONF_KERNEL - 1) + j
                    base = r0 + off - off % SUBLANES
                    acc = acc + xa_ref[off % SUBLANES, base:base + CONV_ROWS, ll] * cw_ref[j:j + 1, ls]
                ya_ref[r0:r0 + CONV_ROWS, ls] = acc + cbias_ref[:, ls]
                acc = jnp.zeros((CONV_ROWS, LANES), F32)
                for j in range(SC_KERNEL):
                    off = SC_HALO - (SC_KERNEL - 1) + j
                    acc = acc + xc_ref[r0 + off:r0 + off + CONV_ROWS, ls] * scw_ref[j:j + 1, ls]
                yc_ref[r0:r0 + CONV_ROWS, ls] = acc
            return carry

        lax.fori_loop(0, half_w // LANES, conv_lanes, 0)

    ya = ya_ref[...]
    mu = jnp.mean(ya, axis=-1, keepdims=True)
    yc0 = ya - mu
    rs = lax.rsqrt(jnp.mean(yc0 * yc0, axis=-1, keepdims=True) + NORM_EPS)
    ln = yc0 * rs * lng_ref[...] + lnb_ref[...]
    act = (ln * jax.nn.sigmoid(ln)).astype(BF16)
    y_conf = jnp.dot(act, wconf_ref[...], preferred_element_type=F32)
    y_sc = jnp.dot((cb_ref[0].astype(F32) * yc_ref[...]).astype(BF16), wsc_ref[...], preferred_element_type=F32)
    y_attn = jnp.dot(o_ref[0], wattn_ref[...], preferred_element_type=F32)
    g = g_ref[0]
    merged = (g[:, 0:D_MODEL].astype(F32) * y_conf + g[:, D_MODEL:2 * D_MODEL].astype(F32) * y_attn
              + g[:, 2 * D_MODEL:3 * D_MODEL].astype(F32) * y_sc)
    out_ref[0] = x_ref[0] + jnp.dot(merged.astype(BF16), wo_ref[...], preferred_element_type=F32)


def _mix(x3, a_act, ccx, cb, gates, o, conf_w, conf_b, ln_g, ln_b, w_conf, sc_w, w_sc, w_attn, w_o, tm):
    b, seq, _ = x3.shape
    nt = seq // tm

    def tile(n):
        return pl.BlockSpec((1, tm, n), lambda bi, ti: (bi, ti, 0))

    def halo(rows):
        per = tm // rows
        return pl.BlockSpec((1, rows, D_MODEL), lambda bi, ti: (bi, jnp.maximum(ti * per - 1, 0), 0))

    def const(shape):
        return pl.BlockSpec(shape, lambda bi, ti: (0,) * len(shape), pipeline_mode=pl.Buffered(1))

    sq = (D_MODEL, D_MODEL)
    return pl.pallas_call(
        functools.partial(_mix_kernel, tm=tm),
        grid=(b, nt),
        in_specs=[tile(D_MODEL), tile(D_MODEL), halo(CONF_HALO), tile(D_MODEL), halo(SC_HALO), tile(D_MODEL),
                  tile(3 * D_MODEL), tile(Q_COLS),
                  const((CONF_KERNEL, D_MODEL)), const((1, D_MODEL)), const((1, D_MODEL)), const((1, D_MODEL)),
                  const(sq), const((SC_KERNEL, D_MODEL)), const(sq), const(sq), const(sq)],
        out_specs=tile(D_MODEL),
        out_shape=jax.ShapeDtypeStruct((b, seq, D_MODEL), F32),
        scratch_shapes=[pltpu.VMEM((SUBLANES, CONF_HALO + tm, D_MODEL // 2), F32),
                        pltpu.VMEM((SC_HALO + tm, D_MODEL), F32),
                        pltpu.VMEM((tm, D_MODEL), F32), pltpu.VMEM((tm, D_MODEL), F32)],
        compiler_params=_cparams("parallel", "parallel"),
        name="mix",
    )(x3, a_act, a_act, ccx, ccx, cb, gates, o, conf_w, conf_b, ln_g, ln_b, w_conf, sc_w, w_sc, w_attn, w_o)


FFN_CHUNK = 256
FFN_CARRY = 8


def _ffn_kernel(x_ref, g2_ref, wg_ref, wv_ref, cwg_ref, cwv_ref, cbg_ref, cbv_ref, wd_ref, out_ref,
                h_ref, ug_ref, uv_ref, act_ref, carry_g, carry_v, *, tm, tiles_per_seq):
    first = (pl.program_id(0) % tiles_per_seq) == 0
    x = x_ref[...]
    h_ref[...] = _rms_rows(x, g2_ref[...]).astype(BF16)

    nf = FFN_DIM // FFN_CHUNK

    def chunk_cols(f):
        return pl.ds(pl.multiple_of(f * FFN_CHUNK, FFN_CHUNK), FFN_CHUNK)

    def up_stage(f, slot):
        cols = chunk_cols(f)
        h = h_ref[...]
        ug_ref[slot, FFN_CARRY:, :] = jnp.dot(h, wg_ref[:, cols], preferred_element_type=F32)
        uv_ref[slot, FFN_CARRY:, :] = jnp.dot(h, wv_ref[:, cols], preferred_element_type=F32)

    def conv_branch(f, slot, cols, u_ref, carry_ref, cw_ref, cb_ref):
        u_ref[slot, 0:FFN_CARRY, :] = jnp.where(first, 0.0, carry_ref[f])
        carry_ref[f] = u_ref[slot, tm:tm + FFN_CARRY, :]
        out = cb_ref[:, cols]
        for j in range(FFN_KERNEL):
            off = FFN_CARRY - (FFN_KERNEL - 1) + j
            out = out + u_ref[slot, pl.ds(off, tm), :] * cw_ref[j:j + 1, cols]
        return out

    def gate_stage(f, slot):
        cols = chunk_cols(f)
        ug = conv_branch(f, slot, cols, ug_ref, carry_g, cwg_ref, cbg_ref)
        uv = conv_branch(f, slot, cols, uv_ref, carry_v, cwv_ref, cbv_ref)
        act_ref[:, cols] = (ug * jax.nn.sigmoid(ug) * uv).astype(BF16)

    up_stage(0, 0)

    def chunk_pair(i, carry):
        f = 2 * i
        up_stage(f + 1, 1)
        gate_stage(f, 0)
        up_stage(f + 2, 0)
        gate_stage(f + 1, 1)
        return carry

    assert nf % 2 == 1
    lax.fori_loop(0, nf // 2, chunk_pair, 0)
    gate_stage(nf - 1, 0)
    out_ref[...] = x + jnp.dot(act_ref[...], wd_ref[...], preferred_element_type=F32)


def _ffn(x2, g2, w_up_g, w_up_v, cw_g, cw_v, cb_g, cb_v, w_down, seq, tm):
    m = x2.shape[0]
    nf = FFN_DIM // FFN_CHUNK

    def const(shape):
        return pl.BlockSpec(shape, lambda mi: (0, 0), pipeline_mode=pl.Buffered(1))

    return pl.pallas_call(
        functools.partial(_ffn_kernel, tm=tm, tiles_per_seq=seq // tm),
        grid=(m // tm,),
        in_specs=[pl.BlockSpec((tm, D_MODEL), lambda mi: (mi, 0)),
                  const((1, D_MODEL)),
                  const((D_MODEL, FFN_DIM)), const((D_MODEL, FFN_DIM)),
                  const((FFN_KERNEL, FFN_DIM)), const((FFN_KERNEL, FFN_DIM)),
                  const((1, FFN_DIM)), const((1, FFN_DIM)),
                  const((FFN_DIM, D_MODEL))],
        out_specs=pl.BlockSpec((tm, D_MODEL), lambda mi: (mi, 0)),
        out_shape=jax.ShapeDtypeStruct((m, D_MODEL), F32),
        scratch_shapes=[pltpu.VMEM((tm, D_MODEL), BF16),
                        pltpu.VMEM((2, FFN_CARRY + tm, FFN_CHUNK), F32),
                        pltpu.VMEM((2, FFN_CARRY + tm, FFN_CHUNK), F32),
                        pltpu.VMEM((tm, FFN_DIM), BF16),
                        pltpu.VMEM((nf, FFN_CARRY, FFN_CHUNK), F32), pltpu.VMEM((nf, FFN_CARRY, FFN_CHUNK), F32)],
        compiler_params=_cparams("arbitrary"),
        name="ffn",
    )(x2, g2, w_up_g, w_up_v, cw_g, cw_v, cb_g, cb_v, w_down)


def _rope_layout(dim, rot_dim):
    half = rot_dim // 2
    rest = list(range(rot_dim, dim))
    lo_fill = LANES // 2 - half
    lanes = list(range(half)) + rest[:lo_fill] + [-1] * max(0, lo_fill - len(rest))
    lanes += list(range(half, rot_dim)) + rest[lo_fill:]
    return lanes + [-1] * (LANES - len(lanes))


def _to_layout(w, layout):
    idx = jnp.asarray([max(i, 0) for i in layout])
    keep = jnp.asarray([1.0 if i >= 0 else 0.0 for i in layout], w.dtype)
    return jnp.take(w, idx, axis=-1) * keep


def _rope_lane_tables(seq, rot_dim):
    half = rot_dim // 2
    pos = jnp.arange(seq, dtype=F32)
    inv_freq = jnp.power(ROPE_THETA, -jnp.arange(0, rot_dim, 2, dtype=F32) / rot_dim)
    ang = pos[:, None] * inv_freq[None, :]
    cos, sin = jnp.cos(ang), jnp.sin(ang)
    one = jnp.ones((seq, LANES // 2 - half), F32)
    zero = jnp.zeros((seq, LANES // 2 - half), F32)
    c = jnp.concatenate([cos, one, cos, one], axis=1)
    s = jnp.concatenate([-sin, zero, sin, zero], axis=1)
    return c, s


def _pad_cols(w, n):
    return jnp.pad(w, ((0, 0), (0, 0), (0, n - w.shape[-1])))


def _tile_rows(seq, target):
    t = min(target, seq)
    assert seq % t == 0
    return t


def kernel(x, norm1_g, w_in, q_norm_g, k_norm_g, w_attn_out, conf_conv_w, conf_conv_b, conf_ln_g, conf_ln_b,
           w_conf_out, sc_conv_w, w_sc_out, w_o, norm2_g, w_up, ffn_conv_w, ffn_conv_b, w_down):
    b, seq, d = x.shape
    depth = w_in.shape[0]
    assert d == D_MODEL and seq % LANES == 0
    tq = _tile_rows(seq, 256)
    tk = _tile_rows(seq, 512)
    tm_proj_attn = _tile_rows(seq, 512)
    tm_proj = _tile_rows(seq, 1024)
    tm_mix = _tile_rows(seq, 512)
    tm_ffn = _tile_rows(seq, 1024)
    ksel = min(TOPK_MAX, seq // 4)

    rot_a, rot_i = HEAD_DIM // ROPE_FRACTION_DIV, IDX_DIM // ROPE_FRACTION_DIV
    tabs_a = _rope_lane_tables(seq, rot_a)
    tabs_i = _rope_lane_tables(seq, rot_i)
    lay_a, lay_i = _rope_layout(HEAD_DIM, rot_a), _rope_layout(IDX_DIM, rot_i)

    def heads_to_layout(w, n_heads, dim, layout):
        return _to_layout(w.reshape(w.shape[:-1] + (n_heads, dim)), layout).reshape(w.shape[:-1] + (n_heads * LANES,))

    offs = [0]
    for s in IN_SIZES:
        offs.append(offs[-1] + s)
    wq, wk, wv, wiq, wik, wiw, wa, wc, wg = (w_in[:, :, offs[i]:offs[i + 1]] for i in range(len(IN_SIZES)))
    w_att = jnp.concatenate([heads_to_layout(wq, N_HEADS, HEAD_DIM, lay_a),
                             heads_to_layout(wk, N_KV_HEADS, HEAD_DIM, lay_a),
                             heads_to_layout(wiq, IDX_HEADS, IDX_DIM, lay_i),
                             _to_layout(wik, lay_i), _pad_cols(wiw, LANES)], axis=-1).astype(BF16)
    layers = dict(
        g1=norm1_g[:, None, :], w_att=w_att, w_vt=jnp.swapaxes(wv, 1, 2).astype(BF16),
        w_a=wa.astype(BF16), w_c=wc.astype(BF16), w_g=wg.astype(BF16),
        qg=_to_layout(q_norm_g, lay_a)[:, None, :], kg=_to_layout(k_norm_g, lay_a)[:, None, :],
        conf_w=conf_conv_w, conf_b=conf_conv_b[:, None, :], ln_g=conf_ln_g[:, None, :], ln_b=conf_ln_b[:, None, :],
        w_conf=w_conf_out.astype(BF16), sc_w=sc_conv_w, w_sc=w_sc_out.astype(BF16),
        w_attn=w_attn_out.astype(BF16), w_o=w_o.astype(BF16),
        g2=norm2_g[:, None, :],
        w_up_g=w_up[:, :, :FFN_DIM].astype(BF16), w_up_v=w_up[:, :, FFN_DIM:].astype(BF16),
        cw_g=ffn_conv_w[:, :, :FFN_DIM], cw_v=ffn_conv_w[:, :, FFN_DIM:],
        cb_g=ffn_conv_b[:, None, :FFN_DIM], cb_v=ffn_conv_b[:, None, FFN_DIM:],
        w_down=w_down.astype(BF16),
    )

    def layer(xc, p):
        x2 = xc.reshape(b * seq, d)
        q, k, vt, iq, ik, iw, a_act, cb, ccx, gates = _input_projections(
            x2, p["g1"], p["w_att"], p["w_vt"], p["w_a"], p["w_c"], p["w_g"], p["qg"], p["kg"],
            tabs_a, tabs_i, seq, tm_proj_attn, tm_proj, tk)
        r3 = lambda t: t.reshape(b, seq, t.shape[-1])
        o = _sparse_attention(r3(iq), r3(ik), r3(iw), r3(q), r3(k), vt.reshape(b, seq // tk, N_KV_HEADS * VT_ROWS, tk),
                              tq, tk, ksel)
        x1 = _mix(xc, r3(a_act), r3(ccx), r3(cb), r3(gates), o, p["conf_w"], p["conf_b"], p["ln_g"], p["ln_b"],
                  p["w_conf"], p["sc_w"], p["w_sc"], p["w_attn"], p["w_o"], tm_mix)
        x_out = _ffn(x1.reshape(b * seq, d), p["g2"], p["w_up_g"], p["w_up_v"], p["cw_g"], p["cw_v"],
                     p["cb_g"], p["cb_v"], p["w_down"], seq, tm_ffn)
        return x_out.reshape(b, seq, d), None

    out, _ = lax.scan(layer, x, layers)
    return out
```
